```python
import math
import jax, jax.numpy as jnp
from jax import lax
import numpy as np

D_MODEL = 2048
BATCH = 4
SEQ = 2048
DEPTH = 4
DEC_BATCH = 128
DEC_SEQ = 1
PAST_LEN = 16384
PAGE_SIZE = 128

N_MIXERS = 2
N_A_LAYERS = (DEPTH + 1) // 2
N_B_LAYERS = DEPTH // 2
PLE_DIM = 256
GMLP_CHUNK = 128
GMLP_WIDTH = 2 * D_MODEL
GMLP_GROUPS = 8
GMLP_GROUP_DIM = GMLP_WIDTH // GMLP_GROUPS
RET_HEADS = 8
RET_DK = D_MODEL // RET_HEADS
RET_DV = 2 * RET_DK
RET_HK = RET_HEADS * RET_DK
RET_HV = RET_HEADS * RET_DV
RET_CHUNK = 128
ROT_BASE = 10000.0
D_FF = ((8 * D_MODEL // 3 + 127) // 128) * 128
CONV_W = 3
EPS = 1e-6

kernel_name = "hybrid_gmlp_retention_convffn_step"


def rmsnorm(x, g):
    xf = x.astype(jnp.float32)
    y = xf * lax.rsqrt(jnp.mean(xf * xf, axis=-1, keepdims=True) + EPS)
    return (y * g.astype(jnp.float32)).astype(x.dtype)


def layernorm(x, g, b):
    xf = x.astype(jnp.float32)
    mu = jnp.mean(xf, axis=-1, keepdims=True)
    var = jnp.mean(jnp.square(xf - mu), axis=-1, keepdims=True)
    y = (xf - mu) * lax.rsqrt(var + EPS)
    return (y * g.astype(jnp.float32) + b.astype(jnp.float32)).astype(x.dtype)


def gmlp_mixer(h, w_in, ln_g, ln_b, w_s, b_s, w_out):
    B, L, _ = h.shape
    z = jax.nn.gelu(h @ w_in)
    u, v = jnp.split(z, 2, axis=-1)
    v = layernorm(v, ln_g, ln_b)
    c = min(L, GMLP_CHUNK)
    n = -(-L // c)
    vp = jnp.pad(v, ((0, 0), (0, n * c - L), (0, 0))).reshape(B, n, c, GMLP_GROUPS, GMLP_GROUP_DIM)
    causal = jnp.tril(jnp.ones((c, c), dtype=bool))
    ws = jnp.where(causal[None], w_s[:, :c, :c], 0.0)
    s = jnp.einsum("gij,bnjgd->bnigd", ws, vp) + b_s[:, :c].T[None, None, :, :, None]
    s = s.reshape(B, n * c, GMLP_WIDTH)[:, :L]
    return (u * s) @ w_out, v


def theta_shift(x, pos):
    angle = 1.0 / (ROT_BASE ** jnp.linspace(0.0, 1.0, RET_DK // 2, dtype=jnp.float32))
    ph = pos.astype(jnp.float32)[:, None] * angle[None, :]
    cos = jnp.cos(ph)[None, :, None, :]
    sin = jnp.sin(ph)[None, :, None, :]
    x1 = x[..., 0::2]
    x2 = x[..., 1::2]
    return jnp.stack([x1 * cos - x2 * sin, x2 * cos + x1 * sin], axis=-1).reshape(x.shape)


def retention_mixer(h, pos, state0, w_in, w_out):
    B, L, _ = h.shape
    f32 = jnp.float32
    q, k, v, g = jnp.split(h @ w_in, [RET_HK, 2 * RET_HK, 2 * RET_HK + RET_HV], axis=-1)
    q = theta_shift(q.reshape(B, L, RET_HEADS, RET_DK).astype(f32), pos)
    k = theta_shift(k.reshape(B, L, RET_HEADS, RET_DK).astype(f32), pos) * (RET_DK ** -0.5)
    v = v.reshape(B, L, RET_HEADS, RET_DV).astype(f32)
    log_gamma = jnp.log(1.0 - 2.0 ** (-5.0 - jnp.arange(RET_HEADS, dtype=f32)))
    c = math.gcd(L, RET_CHUNK)
    n = L // c
    idx = jnp.arange(c, dtype=f32)
    diff = idx[:, None] - idx[None, :]
    decay_in = jnp.where(diff[None] >= 0,
                         jnp.exp(log_gamma[:, None, None] * jnp.maximum(diff, 0.0)[None]), 0.0)
    q_dec = jnp.exp(log_gamma[None, :] * (idx + 1.0)[:, None])[None, :, :, None]
    k_dec = jnp.exp(log_gamma[None, :] * (c - 1.0 - idx)[:, None])[None, :, :, None]
    chunk_dec = jnp.exp(log_gamma * c)[None, :, None, None]

    def to_chunks(t):
        return jnp.moveaxis(t.reshape(B, n, c, RET_HEADS, t.shape[-1]), 1, 0)

    def step(S, qkv):
        qc, kc, vc = qkv
        scores = jnp.einsum("bihd,bjhd->bhij", qc, kc) * decay_in[None]
        inner = jnp.einsum("bhij,bjhe->bihe", scores, vc)
        cross = jnp.einsum("bihd,bhde->bihe", qc * q_dec, S)
        S_new = chunk_dec * S + jnp.einsum("bjhd,bjhe->bhde", kc * k_dec, vc)
        return S_new, inner + cross

    S_fin, o = lax.scan(step, state0.astype(f32), (to_chunks(q), to_chunks(k), to_chunks(v)))
    o = jnp.moveaxis(o, 0, 1).reshape(B, L, RET_HEADS, RET_DV)
    mu = jnp.mean(o, axis=-1, keepdims=True)
    var = jnp.mean(jnp.square(o - mu), axis=-1, keepdims=True)
    o = ((o - mu) * lax.rsqrt(var + EPS)).reshape(B, L, RET_HV).astype(h.dtype)
    return (jax.nn.silu(g) * o) @ w_out, S_fin.astype(state0.dtype)


def conv_ffn(h, buf, w_in, conv_w, conv_b, w_down):
    L = h.shape[1]
    a, b = jnp.split(h @ w_in, 2, axis=-1)
    ext = jnp.concatenate([buf.astype(a.dtype), a], axis=1)
    conv = conv_b + conv_w[0] * ext[:, 0:L]
    for j in range(1, CONV_W):
        conv = conv + conv_w[j] * ext[:, j:j + L]
    return (jax.nn.gelu(conv) * b) @ w_down, ext[:, -(CONV_W - 1):]


def setup_inputs(seed: int = 0) -> dict:
    key = jax.random.key(seed)
    ks = iter(jax.random.split(key, 32))

    def nrm(shape, scale):
        return scale * jax.random.normal(next(ks), shape, jnp.float32)

    def gain(shape):
        return 1.0 + nrm(shape, 0.02)

    return {
        "x_prompt": nrm((BATCH, SEQ, D_MODEL), 1.0),
        "x_sample": nrm((DEC_BATCH, DEC_SEQ, D_MODEL), 1.0),
        "state_ret": nrm((N_B_LAYERS, DEC_BATCH, RET_HEADS, RET_DK, RET_DV), RET_DK ** -0.5),
        "state_conv": nrm((DEPTH, DEC_BATCH, CONV_W - 1, D_FF), 1.0),
        "p_prompt": nrm((DEPTH, BATCH, SEQ, PLE_DIM), 1.0),
        "p_sample": nrm((DEPTH, DEC_BATCH, DEC_SEQ, PLE_DIM), 1.0),
        "norm_mix_g": gain((DEPTH, D_MODEL)),
        "norm_ffn_g": gain((DEPTH, D_MODEL)),
        "norm_ple_g": gain((DEPTH, D_MODEL)),
        "final_norm_g": gain((D_MODEL,)),
        "gmlp_w_in": nrm((N_A_LAYERS, D_MODEL, 2 * GMLP_WIDTH), D_MODEL ** -0.5),
        "gmlp_ln_g": gain((N_A_LAYERS, GMLP_WIDTH)),
        "gmlp_ln_b": nrm((N_A_LAYERS, GMLP_WIDTH), 0.02),
        "gmlp_w_s": nrm((N_A_LAYERS, GMLP_GROUPS, GMLP_CHUNK, GMLP_CHUNK), GMLP_CHUNK ** -0.5),
        "gmlp_b_s": 1.0 + nrm((N_A_LAYERS, GMLP_GROUPS, GMLP_CHUNK), 0.1),
        "gmlp_w_out": nrm((N_A_LAYERS, GMLP_WIDTH, D_MODEL), GMLP_WIDTH ** -0.5),
        "ret_w_in": nrm((N_B_LAYERS, D_MODEL, 2 * RET_HK + 2 * RET_HV), D_MODEL ** -0.5),
        "ret_w_out": nrm((N_B_LAYERS, RET_HV, D_MODEL), RET_HV ** -0.5),
        "ffn_w_in": nrm((DEPTH, D_MODEL, 2 * D_FF), D_MODEL ** -0.5),
        "ffn_conv_w": nrm((DEPTH, CONV_W, D_FF), CONV_W ** -0.5),
        "ffn_conv_b": nrm((DEPTH, D_FF), 0.02),
        "ffn_w_down": nrm((DEPTH, D_FF, D_MODEL), D_FF ** -0.5),
        "ple_w_proj": nrm((DEPTH, PLE_DIM, D_MODEL), PLE_DIM ** -0.5),
        "ple_w_gate": nrm((DEPTH, D_MODEL, D_MODEL), D_MODEL ** -0.5),
    }


def reference(x_prompt, x_sample, state_ret, state_conv, p_prompt, p_sample,
              norm_mix_g, norm_ffn_g, norm_ple_g, final_norm_g,
              gmlp_w_in, gmlp_ln_g, gmlp_ln_b, gmlp_w_s, gmlp_b_s, gmlp_w_out,
              ret_w_in, ret_w_out,
              ffn_w_in, ffn_conv_w, ffn_conv_b, ffn_w_down,
              ple_w_proj, ple_w_gate):

    def trunk(x, p, pos, ret_state0, conv_state0, keep_chunk_rows):
        h = x
        ret_new, conv_new, v_new = [], [], []
        for i in range(DEPTH):
            hn = rmsnorm(h, norm_mix_g[i])
            li = i // N_MIXERS
            if i % N_MIXERS == 0:
                y, v_rows = gmlp_mixer(hn, gmlp_w_in[li], gmlp_ln_g[li], gmlp_ln_b[li],
                                       gmlp_w_s[li], gmlp_b_s[li], gmlp_w_out[li])
                if keep_chunk_rows:
                    v_new.append(v_rows)
            else:
                y, S = retention_mixer(hn, pos, ret_state0[li], ret_w_in[li], ret_w_out[li])
                ret_new.append(S)
            h = h + y
            y, buf = conv_ffn(rmsnorm(h, norm_ffn_g[i]), conv_state0[i],
                              ffn_w_in[i], ffn_conv_w[i], ffn_conv_b[i], ffn_w_down[i])
            conv_new.append(buf)
            h = h + y
            gate = jax.nn.sigmoid(rmsnorm(h, norm_ple_g[i]) @ ple_w_gate[i])
            h = h + gate * (p[i] @ ple_w_proj[i])
        v_out = jnp.stack(v_new) if keep_chunk_rows else None
        return rmsnorm(h, final_norm_g), jnp.stack(ret_new), jnp.stack(conv_new), v_out

    bp, lp = x_prompt.shape[0], x_prompt.shape[1]
    pos_prompt = jnp.arange(lp, dtype=jnp.int32)
    pos_sample = PAST_LEN + jnp.arange(x_sample.shape[1], dtype=jnp.int32)
    ret0_prompt = jnp.zeros((N_B_LAYERS, bp, RET_HEADS, RET_DK, RET_DV), x_prompt.dtype)
    conv0_prompt = jnp.zeros((DEPTH, bp, CONV_W - 1, D_FF), x_prompt.dtype)

    y_prompt, ret_state_prompt, conv_state_prompt, _ = trunk(
        x_prompt, p_prompt, pos_prompt, ret0_prompt, conv0_prompt, False)
    y_sample, ret_state_sample, conv_state_sample, gmlp_v_sample = trunk(
        x_sample, p_sample, pos_sample, state_ret, state_conv, True)

    return (y_prompt, y_sample, ret_state_prompt, ret_state_sample,
            conv_state_prompt, conv_state_sample, gmlp_v_sample)
```

```python
import functools
import math

import jax
import jax.numpy as jnp
from jax import lax
from jax.experimental import pallas as pl
from jax.experimental.pallas import tpu as pltpu

F32 = jnp.float32
BF16 = jnp.bfloat16

D_MODEL = 2048
DEPTH = 4
PAST_LEN = 16384
N_MIXERS = 2
PLE_DIM = 256
GMLP_CHUNK = 128
GMLP_WIDTH = 2 * D_MODEL
GMLP_GROUPS = 8
GMLP_GROUP_DIM = GMLP_WIDTH // GMLP_GROUPS
RET_HEADS = 8
RET_DK = D_MODEL // RET_HEADS
RET_DV = 2 * RET_DK
RET_HK = RET_HEADS * RET_DK
RET_HV = RET_HEADS * RET_DV
RET_CHUNK = 128
ROT_BASE = 10000.0
D_FF = ((8 * D_MODEL // 3 + 127) // 128) * 128
CONV_W = 3
EPS = 1e-6

V7X_VMEM_BYTES = 64 * 1024 * 1024
VMEM_LIMIT_BYTES = V7X_VMEM_BYTES - 8 * 1024 * 1024
SUBLANES = 8
LANES = 128
MXU_DEPTH = 256
D_FF_PAD = -(-D_FF // (2 * MXU_DEPTH)) * (2 * MXU_DEPTH)
FFN_TN = 512
FFN_ROW_CHUNK = 256


def _params(*sem):
    return pltpu.CompilerParams(dimension_semantics=sem, vmem_limit_bytes=VMEM_LIMIT_BYTES)


def _rms(x, g):
    r = lax.rsqrt(jnp.mean(x * x, axis=-1, keepdims=True) + EPS)
    return x * r * g


def _group_norm(o):
    mu = jnp.mean(o, axis=-1, keepdims=True)
    d = o - mu
    var = jnp.mean(d * d, axis=-1, keepdims=True)
    return d * lax.rsqrt(var + EPS)


def _rotate(x, cos, sin_even, sin_odd):
    n = x.shape[-1]
    return x * cos + pltpu.roll(x, n - 1, 1) * sin_even + pltpu.roll(x, 1, 1) * sin_odd


def _rms_kernel(x_ref, g_ref, o_ref):
    o_ref[...] = _rms(x_ref[...], g_ref[...]).astype(o_ref.dtype)


def rmsnorm_rows(x, g, out_dtype):
    t, d = x.shape
    tm = min(256, t)
    return pl.pallas_call(
        _rms_kernel,
        out_shape=jax.ShapeDtypeStruct((t, d), out_dtype),
        grid=(t // tm,),
        in_specs=[pl.BlockSpec((tm, d), lambda i: (i, 0)), pl.BlockSpec((1, d), lambda i: (0, 0))],
        out_specs=pl.BlockSpec((tm, d), lambda i: (i, 0)),
        compiler_params=_params("parallel"),
        name="rmsnorm_rows",
    )(x, g.reshape(1, d))


def _mm_kernel(x_ref, w_ref, o_ref, *, act):
    acc = jnp.dot(x_ref[...], w_ref[...], preferred_element_type=F32)
    if act == "gelu":
        acc = jax.nn.gelu(acc)
    o_ref[...] = acc.astype(o_ref.dtype)


def matmul_act(x, w, act, name):
    t, k = x.shape
    n = w.shape[1]
    tm = min(1024, t)
    tn = 1024
    return pl.pallas_call(
        functools.partial(_mm_kernel, act=act),
        out_shape=jax.ShapeDtypeStruct((t, n), F32),
        grid=(t // tm, n // tn),
        in_specs=[pl.BlockSpec((tm, k), lambda i, j: (i, 0)), pl.BlockSpec((k, tn), lambda i, j: (0, j))],
        out_specs=pl.BlockSpec((tm, tn), lambda i, j: (i, j)),
        compiler_params=_params("parallel", "arbitrary"),
        name=name,
    )(x, w)


def _layernorm(v, g, b):
    mu = jnp.mean(v, axis=-1, keepdims=True)
    d = v - mu
    var = jnp.mean(d * d, axis=-1, keepdims=True)
    return d * lax.rsqrt(var + EPS) * g + b


def _gmlp_gate_kernel(u_ref, v_ref, lng_ref, lnb_ref, ws_ref, bs_ref, x_ref):
    vn = _layernorm(v_ref[...], lng_ref[...], lnb_ref[...]).astype(BF16)
    c = GMLP_CHUNK
    causal = lax.broadcasted_iota(jnp.int32, (c, c), 0) >= lax.broadcasted_iota(jnp.int32, (c, c), 1)
    for g in range(GMLP_GROUPS):
        cols = slice(g * GMLP_GROUP_DIM, (g + 1) * GMLP_GROUP_DIM)
        w = jnp.where(causal, ws_ref[g], 0.0).astype(BF16)
        s = jnp.dot(w, vn[:, cols], preferred_element_type=F32) + bs_ref[g]
        x_ref[:, cols] = (u_ref[:, cols] * s).astype(x_ref.dtype)


def gmlp_gate_prompt(z, ln_g, ln_b, w_s, b_s):
    t = z.shape[0]
    w = GMLP_WIDTH
    c = GMLP_CHUNK
    bs_b = jnp.broadcast_to(b_s[:, :, None], (GMLP_GROUPS, c, GMLP_GROUP_DIM))
    return pl.pallas_call(
        _gmlp_gate_kernel,
        out_shape=jax.ShapeDtypeStruct((t, w), BF16),
        grid=(t // c,),
        in_specs=[
            pl.BlockSpec((c, w), lambda i: (i, 0)),
            pl.BlockSpec((c, w), lambda i: (i, 1)),
            pl.BlockSpec((1, w), lambda i: (0, 0)),
            pl.BlockSpec((1, w), lambda i: (0, 0)),
            pl.BlockSpec((GMLP_GROUPS, c, c), lambda i: (0, 0, 0)),
            pl.BlockSpec((GMLP_GROUPS, c, GMLP_GROUP_DIM), lambda i: (0, 0, 0)),
        ],
        out_specs=pl.BlockSpec((c, w), lambda i: (i, 0)),
        compiler_params=_params("parallel"),
        name="gmlp_gate_prompt",
    )(z, z, ln_g.reshape(1, w), ln_b.reshape(1, w), w_s, bs_b)


def _gmlp_gate_sample_kernel(u_ref, v_ref, lng_ref, lnb_ref, scale_ref, shift_ref, x_ref, vn_ref):
    vn = _layernorm(v_ref[...], lng_ref[...], lnb_ref[...])
    vn_ref[...] = vn
    x_ref[...] = (u_ref[...] * (vn * scale_ref[...] + shift_ref[...])).astype(x_ref.dtype)


def gmlp_gate_sample(z, ln_g, ln_b, w_s, b_s):
    t = z.shape[0]
    w = GMLP_WIDTH
    scale = jnp.repeat(w_s[:, 0, 0], GMLP_GROUP_DIM).reshape(1, w)
    shift = jnp.repeat(b_s[:, 0], GMLP_GROUP_DIM).reshape(1, w)
    row = pl.BlockSpec((1, w), lambda i: (0, 0))
    return pl.pallas_call(
        _gmlp_gate_sample_kernel,
        out_shape=(jax.ShapeDtypeStruct((t, w), BF16), jax.ShapeDtypeStruct((t, w), F32)),
        grid=(1,),
        in_specs=[pl.BlockSpec((t, w), lambda i: (0, 0)), pl.BlockSpec((t, w), lambda i: (0, 1)), row, row, row, row],
        out_specs=(pl.BlockSpec((t, w), lambda i: (0, 0)), pl.BlockSpec((t, w), lambda i: (0, 0))),
        compiler_params=_params("arbitrary"),
        name="gmlp_gate_sample",
    )(z, z, ln_g.reshape(1, w), ln_b.reshape(1, w), scale, shift)


def _rotary_tables(pos):
    angle = 1.0 / (ROT_BASE ** jnp.linspace(0.0, 1.0, RET_DK // 2, dtype=F32))
    ph = pos.astype(F32)[:, None] * angle[None, :]
    cos = jnp.cos(ph)
    sin = jnp.sin(ph)
    zero = jnp.zeros_like(sin)
    n = pos.shape[0]
    cos_r = jnp.stack([cos, cos], axis=-1).reshape(n, RET_DK)
    sin_even = jnp.stack([-sin, zero], axis=-1).reshape(n, RET_DK)
    sin_odd = jnp.stack([zero, sin], axis=-1).reshape(n, RET_DK)
    return cos_r, sin_even, sin_odd


def _log_gamma():
    return jnp.log(1.0 - 2.0 ** (-5.0 - jnp.arange(RET_HEADS, dtype=F32)))


def _ret_prompt_kernel(cdec_ref, q_ref, k_ref, v_ref, g_ref, cos_ref, se_ref, so_ref,
                       din_ref, qd_ref, kd_ref, x_ref, s_ref):
    @pl.when(pl.program_id(1) == 0)
    def _():
        s_ref[...] = jnp.zeros_like(s_ref)

    cos = cos_ref[...]
    se = se_ref[...]
    so = so_ref[...]
    for h in range(RET_HEADS):
        kc = slice(h * RET_DK, (h + 1) * RET_DK)
        vc = slice(h * RET_DV, (h + 1) * RET_DV)
        q = _rotate(q_ref[:, kc], cos, se, so)
        k = _rotate(k_ref[:, kc], cos, se, so) * (RET_DK ** -0.5)
        v = v_ref[:, vc].astype(BF16)
        scores = lax.dot_general(q.astype(BF16), k.astype(BF16), (((1,), (1,)), ((), ())),
                                 preferred_element_type=F32) * din_ref[h]
        s_old = s_ref[0, h]
        inner = jnp.dot(scores.astype(BF16), v, preferred_element_type=F32)
        cross = jnp.dot((q * qd_ref[h]).astype(BF16), s_old.astype(BF16), preferred_element_type=F32)
        kt = jnp.transpose(k * kd_ref[h]).astype(BF16)
        s_ref[0, h] = cdec_ref[h] * s_old + jnp.dot(kt, v, preferred_element_type=F32)
        on = _group_norm(inner + cross)
        x_ref[:, vc] = (jax.nn.silu(g_ref[:, vc]) * on).astype(x_ref.dtype)


def retention_prompt(qkvg, batch, seq):
    c = RET_CHUNK
    n = seq // c
    lg = _log_gamma()
    idx = jnp.arange(c, dtype=F32)
    diff = idx[:, None] - idx[None, :]
    decay_in = jnp.where(diff[None] >= 0, jnp.exp(lg[:, None, None] * jnp.maximum(diff, 0.0)[None]), 0.0)
    q_dec = jnp.exp(lg[:, None] * (idx + 1.0)[None, :])
    k_dec = jnp.exp(lg[:, None] * (c - 1.0 - idx)[None, :])
    chunk_dec = jnp.exp(lg * c)
    qd = jnp.broadcast_to(q_dec[:, :, None], (RET_HEADS, c, RET_DK))
    kd = jnp.broadcast_to(k_dec[:, :, None], (RET_HEADS, c, RET_DK))
    cos_r, sin_even, sin_odd = _rotary_tables(jnp.arange(seq, dtype=jnp.int32))

    rows = lambda b, j: b * n + j
    tab = pl.BlockSpec((c, RET_DK), lambda b, j: (j, 0))
    full3 = lambda shape: pl.BlockSpec(shape, lambda b, j: (0, 0, 0))
    return pl.pallas_call(
        _ret_prompt_kernel,
        out_shape=(jax.ShapeDtypeStruct((batch * seq, RET_HV), BF16),
                   jax.ShapeDtypeStruct((batch, RET_HEADS, RET_DK, RET_DV), F32)),
        grid=(batch, n),
        in_specs=[
            pl.BlockSpec(memory_space=pltpu.SMEM),
            pl.BlockSpec((c, RET_HK), lambda b, j: (rows(b, j), 0)),
            pl.BlockSpec((c, RET_HK), lambda b, j: (rows(b, j), 1)),
            pl.BlockSpec((c, RET_HV), lambda b, j: (rows(b, j), 1)),
            pl.BlockSpec((c, RET_HV), lambda b, j: (rows(b, j), 2)),
            tab, tab, tab,
            full3((RET_HEADS, c, c)), full3((RET_HEADS, c, RET_DK)), full3((RET_HEADS, c, RET_DK)),
        ],
        out_specs=(pl.BlockSpec((c, RET_HV), lambda b, j: (rows(b, j), 0)),
                   pl.BlockSpec((1, RET_HEADS, RET_DK, RET_DV), lambda b, j: (b, 0, 0, 0))),
        compiler_params=_params("parallel", "arbitrary"),
        name="retention_prompt",
    )(chunk_dec, qkvg, qkvg, qkvg, qkvg, cos_r, sin_even, sin_odd, decay_in, qd, kd)


def _ret_sample_kernel(cdec_ref, row_ref, cos_ref, se_ref, so_ref, gq_ref, s_ref, x_ref, so_out_ref):
    row = row_ref[0]
    q = _rotate(row[:, 0:RET_HK], cos_ref[...], se_ref[...], so_ref[...])
    k = _rotate(row[:, RET_HK:2 * RET_HK], cos_ref[...], se_ref[...], so_ref[...]) * (RET_DK ** -0.5)
    v = row[:, 2 * RET_HK:2 * RET_HK + RET_HV]
    g = row[:, 2 * RET_HK + RET_HV:]
    qk = q * k

    pad = LANES
    head_of_row = lax.broadcasted_iota(jnp.int32, (pad, RET_HK), 0)
    head_of_col = lax.shift_right_logical(lax.broadcasted_iota(jnp.int32, (pad, RET_HK), 1),
                                          int(math.log2(RET_DK)))
    on_diag = head_of_row == head_of_col
    q_bd = jnp.where(on_diag[:SUBLANES], q * gq_ref[...], 0.0).astype(BF16)
    k_bd = jnp.where(on_diag, k, 0.0)
    s_old = s_ref[0]
    cross = jnp.dot(q_bd, s_old.astype(BF16), preferred_element_type=F32)
    v_row_id = lax.broadcasted_iota(jnp.int32, (pad, RET_DV), 0)
    v_rows = jnp.zeros((pad, RET_DV), F32)
    for h in range(RET_HEADS):
        v_rows = jnp.where(v_row_id == h, v[:, h * RET_DV:(h + 1) * RET_DV], v_rows)
    outer = jnp.dot(jnp.transpose(k_bd).astype(BF16), v_rows.astype(BF16), preferred_element_type=F32)
    for h in range(RET_HEADS):
        kc = slice(h * RET_DK, (h + 1) * RET_DK)
        vc = slice(h * RET_DV, (h + 1) * RET_DV)
        score = jnp.sum(qk[:, kc], axis=-1, keepdims=True)
        on = _group_norm(score * v[:, vc] + cross[h:h + 1, :])
        x_ref[0, :, vc] = (jax.nn.silu(g[:, vc]) * on).astype(x_ref.dtype)
        so_out_ref[0, kc, :] = cdec_ref[h] * s_old[kc, :] + outer[kc, :]


def retention_sample(qkvg, state_all, li):
    b = qkvg.shape[0]
    n_in = qkvg.shape[1]
    n_layers = state_all.shape[0]
    lg = _log_gamma()
    gamma = jnp.exp(lg * 1.0)
    gq = jnp.repeat(gamma, RET_DK).reshape(1, RET_HK)
    cos_r, sin_even, sin_odd = _rotary_tables(PAST_LEN + jnp.arange(1, dtype=jnp.int32))
    tile = lambda t: jnp.tile(t, (1, RET_HEADS))
    vec = pl.BlockSpec((1, RET_HK), lambda i: (0, 0))
    x, s_new = pl.pallas_call(
        _ret_sample_kernel,
        out_shape=(jax.ShapeDtypeStruct((b, 1, RET_HV), BF16),
                   jax.ShapeDtypeStruct((b, RET_HK, RET_DV), F32)),
        grid=(b,),
        in_specs=[
            pl.BlockSpec(memory_space=pltpu.SMEM),
            pl.BlockSpec((1, 1, n_in), lambda i: (i, 0, 0)),
            vec, vec, vec, vec,
            pl.BlockSpec((None, 1, RET_HK, RET_DV), lambda i: (li, i, 0, 0)),
        ],
        out_specs=(pl.BlockSpec((1, 1, RET_HV), lambda i: (i, 0, 0)),
                   pl.BlockSpec((1, RET_HK, RET_DV), lambda i: (i, 0, 0))),
        compiler_params=_params("parallel"),
        name="retention_sample",
    )(gamma, qkvg.reshape(b, 1, n_in), tile(cos_r), tile(sin_even), tile(sin_odd), gq,
      state_all.reshape(n_layers, b, RET_HK, RET_DV))
    return x.reshape(b, RET_HV), s_new.reshape(b, RET_HEADS, RET_DK, RET_DV)


def _proj_res_kernel(x_ref, w_ref, h_ref, g_ref, ho_ref, hn_ref):
    h_new = h_ref[...] + jnp.dot(x_ref[...], w_ref[...], preferred_element_type=F32)
    ho_ref[...] = h_new
    hn_ref[...] = _rms(h_new, g_ref[...]).astype(hn_ref.dtype)


def proj_residual(x, w, h, g_next, name):
    t, k = x.shape
    d = w.shape[1]
    tm = min(256, t)
    return pl.pallas_call(
        _proj_res_kernel,
        out_shape=(jax.ShapeDtypeStruct((t, d), F32), jax.ShapeDtypeStruct((t, d), BF16)),
        grid=(t // tm,),
        in_specs=[
            pl.BlockSpec((tm, k), lambda i: (i, 0)),
            pl.BlockSpec((k, d), lambda i: (0, 0), pipeline_mode=pl.Buffered(1)),
            pl.BlockSpec((tm, d), lambda i: (i, 0)),
            pl.BlockSpec((1, d), lambda i: (0, 0)),
        ],
        out_specs=(pl.BlockSpec((tm, d), lambda i: (i, 0)), pl.BlockSpec((tm, d), lambda i: (i, 0))),
        compiler_params=_params("parallel"),
        name=name,
    )(x, w, h, g_next.reshape(1, d))


def _conv_gate(a, a1, a2, b, cw_ref, cb_ref):
    conv = cb_ref[...] + cw_ref[0:1, :] * a2 + cw_ref[1:2, :] * a1 + cw_ref[2:3, :] * a
    return jax.nn.gelu(conv) * b


def _ffn_in_prompt_kernel(x_ref, wa_ref, wb_ref, cw_ref, cb_ref, y_ref, st_ref):
    seq = x_ref.shape[0]
    rc = FFN_ROW_CHUNK
    tn = wa_ref.shape[1]
    rows = lax.broadcasted_iota(jnp.int32, (rc, tn), 0)
    prev1 = jnp.zeros((1, tn), F32)
    prev2 = jnp.zeros((1, tn), F32)
    for r in range(seq // rc):
        xs = x_ref[r * rc:(r + 1) * rc, :]
        a = jnp.dot(xs, wa_ref[...], preferred_element_type=F32)
        b = jnp.dot(xs, wb_ref[...], preferred_element_type=F32)
        a1 = jnp.where(rows >= 1, pltpu.roll(a, 1, 0), prev1)
        a2 = jnp.where(rows >= 2, pltpu.roll(a, 2, 0), jnp.where(rows == 1, prev1, prev2))
        y_ref[r * rc:(r + 1) * rc, :] = _conv_gate(a, a1, a2, b, cw_ref, cb_ref).astype(y_ref.dtype)
        prev1 = a[rc - 1:rc, :]
        prev2 = a[rc - 2:rc - 1, :]
        if r == seq // rc - 1:
            st_ref[0] = a[rc - SUBLANES:, :]


def ffn_in_prompt(hn, wa, wb, cw, cb, batch, seq):
    tn = FFN_TN
    d = hn.shape[1]
    y, st = pl.pallas_call(
        _ffn_in_prompt_kernel,
        out_shape=(jax.ShapeDtypeStruct((batch * seq, D_FF_PAD), BF16),
                   jax.ShapeDtypeStruct((batch, SUBLANES, D_FF_PAD), F32)),
        grid=(batch, D_FF_PAD // tn),
        in_specs=[
            pl.BlockSpec((seq, d), lambda b, j: (b, 0)),
            pl.BlockSpec((d, tn), lambda b, j: (0, j)),
            pl.BlockSpec((d, tn), lambda b, j: (0, j)),
            pl.BlockSpec((CONV_W, tn), lambda b, j: (0, j)),
            pl.BlockSpec((1, tn), lambda b, j: (0, j)),
        ],
        out_specs=(pl.BlockSpec((seq, tn), lambda b, j: (b, j)),
                   pl.BlockSpec((1, SUBLANES, tn), lambda b, j: (b, 0, j))),
        compiler_params=_params("parallel", "arbitrary"),
        name="ffn_in_prompt",
    )(hn, wa, wb, cw, cb)
    return y, st[:, SUBLANES - (CONV_W - 1):, :D_FF]


def _ffn_in_sample_kernel(x_ref, wa_ref, wb_ref, cw_ref, cb_ref, b0_ref, b1_ref, y_ref, a_ref):
    x = x_ref[...]
    a = jnp.dot(x, wa_ref[...], preferred_element_type=F32)
    b = jnp.dot(x, wb_ref[...], preferred_element_type=F32)
    a_ref[...] = a
    y_ref[...] = _conv_gate(a, b1_ref[...], b0_ref[...], b, cw_ref, cb_ref).astype(y_ref.dtype)


def ffn_in_sample(hn, wa, wb, cw, cb, buf):
    t, d = hn.shape
    tn = FFN_TN
    pad = ((0, 0), (0, D_FF_PAD - D_FF))
    b0 = jnp.pad(buf[:, 0, :], pad)
    b1 = jnp.pad(buf[:, 1, :], pad)
    col = lambda rows_: pl.BlockSpec((rows_, tn), lambda j: (0, j))
    y, a = pl.pallas_call(
        _ffn_in_sample_kernel,
        out_shape=(jax.ShapeDtypeStruct((t, D_FF_PAD), BF16), jax.ShapeDtypeStruct((t, D_FF_PAD), F32)),
        grid=(D_FF_PAD // tn,),
        in_specs=[pl.BlockSpec((t, d), lambda j: (0, 0)), col(d), col(d), col(CONV_W), col(1), col(t), col(t)],
        out_specs=(col(t), col(t)),
        compiler_params=_params("parallel"),
        name="ffn_in_sample",
    )(hn, wa, wb, cw, cb, b0, b1)
    return y, jnp.stack([buf[:, 1, :], a[:, :D_FF]], axis=1)


def _ple_kernel(hn_ref, wg_ref, p_ref, wp_ref, h_ref, g_ref, *out_refs, emit_h):
    gate = jax.nn.sigmoid(jnp.dot(hn_ref[...], wg_ref[...], preferred_element_type=F32))
    proj = jnp.dot(p_ref[...].astype(BF16), wp_ref[...], preferred_element_type=F32)
    h_new = h_ref[...] + gate * proj
    if emit_h:
        out_refs[0][...] = h_new
    out_refs[-1][...] = _rms(h_new, g_ref[...]).astype(out_refs[-1].dtype)


def ple_residual(hn, wg, p_all, layer, wp, h, g_next, emit_h):
    t, d = h.shape
    pd = p_all.shape[-1]
    p_all = p_all.reshape(p_all.shape[0], t, pd)
    tm = min(256, t)
    rows = lambda width: pl.BlockSpec((tm, width), lambda i: (i, 0))
    const = lambda shape: pl.BlockSpec(shape, lambda i: (0, 0), pipeline_mode=pl.Buffered(1))
    if emit_h:
        out_shape = (jax.ShapeDtypeStruct((t, d), F32), jax.ShapeDtypeStruct((t, d), BF16))
        out_specs = (rows(d), rows(d))
    else:
        out_shape = (jax.ShapeDtypeStruct((t, d), F32),)
        out_specs = (rows(d),)
    return pl.pallas_call(
        functools.partial(_ple_kernel, emit_h=emit_h),
        out_shape=out_shape,
        grid=(t // tm,),
        in_specs=[rows(d), const((d, d)), pl.BlockSpec((None, tm, pd), lambda i: (layer, i, 0)), const((pd, d)),
                  rows(d), pl.BlockSpec((1, d), lambda i: (0, 0))],
        out_specs=out_specs,
        compiler_params=_params("parallel"),
        name="ple_residual",
    )(hn, wg, p_all, wp, h, g_next.reshape(1, d))


def _trunk(x, p, batch, seq, ret_state0, conv_state0, w, is_sample):
    t = batch * seq
    h = x.reshape(t, D_MODEL)
    hn = rmsnorm_rows(h, w["norm_mix_g"][0], BF16)
    ret_new, conv_new, v_new = [], [], []
    y_final = None
    for i in range(DEPTH):
        li = i // N_MIXERS
        if i % N_MIXERS == 0:
            z = matmul_act(hn, w["gmlp_w_in"][li], "gelu", "gmlp_in")
            if is_sample:
                xm, vn = gmlp_gate_sample(z, w["gmlp_ln_g"][li], w["gmlp_ln_b"][li], w["gmlp_w_s"][li], w["gmlp_b_s"][li])
                v_new.append(vn.reshape(batch, seq, GMLP_WIDTH))
            else:
                xm = gmlp_gate_prompt(z, w["gmlp_ln_g"][li], w["gmlp_ln_b"][li], w["gmlp_w_s"][li], w["gmlp_b_s"][li])
            w_out = w["gmlp_w_out"][li]
        else:
            qkvg = matmul_act(hn, w["ret_w_in"][li], None, "ret_in")
            if is_sample:
                xm, s_new = retention_sample(qkvg, ret_state0, li)
            else:
                xm, s_new = retention_prompt(qkvg, batch, seq)
            ret_new.append(s_new)
            w_out = w["ret_w_out"][li]
        h, hn = proj_residual(xm, w_out, h, w["norm_ffn_g"][i], "mixer_out")
        if is_sample:
            y, buf = ffn_in_sample(hn, w["ffn_wa"][i], w["ffn_wb"][i], w["ffn_cw"][i], w["ffn_cb"][i], conv_state0[i])
        else:
            y, buf = ffn_in_prompt(hn, w["ffn_wa"][i], w["ffn_wb"][i], w["ffn_cw"][i], w["ffn_cb"][i], batch, seq)
        conv_new.append(buf)
        h, hn = proj_residual(y, w["ffn_w_down"][i], h, w["norm_ple_g"][i], "ffn_down")
        last = i == DEPTH - 1
        g_next = w["final_norm_g"] if last else w["norm_mix_g"][i + 1]
        outs = ple_residual(hn, w["ple_w_gate"][i], p, i, w["ple_w_proj"][i], h, g_next, not last)
        if last:
            y_final = outs[0]
        else:
            h, hn = outs
    v_out = jnp.stack(v_new) if is_sample else None
    return y_final.reshape(batch, seq, D_MODEL), jnp.stack(ret_new), jnp.stack(conv_new), v_out


def kernel(x_prompt, x_sample, state_ret, state_conv, p_prompt, p_sample, norm_mix_g, norm_ffn_g, norm_ple_g, final_norm_g, gmlp_w_in, gmlp_ln_g, gmlp_ln_b, gmlp_w_s, gmlp_b_s, gmlp_w_out, ret_w_in, ret_w_out, ffn_w_in, ffn_conv_w, ffn_conv_b, ffn_w_down, ple_w_proj, ple_w_gate):
    ff_pad = D_FF_PAD - D_FF
    per_layer = lambda a: [a[i].astype(BF16) for i in range(a.shape[0])]
    w = {
        "norm_mix_g": norm_mix_g, "norm_ffn_g": norm_ffn_g, "norm_ple_g": norm_ple_g, "final_norm_g": final_norm_g,
        "gmlp_w_in": per_layer(gmlp_w_in), "gmlp_ln_g": gmlp_ln_g, "gmlp_ln_b": gmlp_ln_b,
        "gmlp_w_s": gmlp_w_s, "gmlp_b_s": gmlp_b_s, "gmlp_w_out": per_layer(gmlp_w_out),
        "ret_w_in": per_layer(ret_w_in), "ret_w_out": per_layer(ret_w_out),
        "ffn_wa": [jnp.pad(ffn_w_in[i, :, :D_FF].astype(BF16), ((0, 0), (0, ff_pad))) for i in range(DEPTH)],
        "ffn_wb": [jnp.pad(ffn_w_in[i, :, D_FF:].astype(BF16), ((0, 0), (0, ff_pad))) for i in range(DEPTH)],
        "ffn_cw": jnp.pad(ffn_conv_w, ((0, 0), (0, 0), (0, ff_pad))),
        "ffn_cb": jnp.pad(ffn_conv_b, ((0, 0), (0, ff_pad))).reshape(DEPTH, 1, D_FF_PAD),
        "ffn_w_down": [jnp.pad(ffn_w_down[i].astype(BF16), ((0, ff_pad), (0, 0))) for i in range(DEPTH)],
        "ple_w_proj": per_layer(ple_w_proj), "ple_w_gate": per_layer(ple_w_gate),
    }
    bp, lp = x_prompt.shape[0], x_prompt.shape[1]
    bs, ls = x_sample.shape[0], x_sample.shape[1]
    y_prompt, ret_prompt, conv_prompt, _ = _trunk(x_prompt, p_prompt, bp, lp, None, None, w, False)
    y_sample, ret_sample, conv_sample, v_sample = _trunk(x_sample, p_sample, bs, ls, state_ret, state_conv, w, True)
    return (y_prompt, y_sample, ret_prompt, ret_sample, conv_prompt, conv_sample, v_sample)
```

```python
import functools
import math

import jax
import jax.numpy as jnp
from jax import lax
from jax.experimental import pallas as pl
from jax.experimental.pallas import tpu as pltpu

F32 = jnp.float32
BF16 = jnp.bfloat16

D_MODEL = 2048
DEPTH = 4
PAST_LEN = 16384
N_MIXERS = 2
PLE_DIM = 256
GMLP_CHUNK = 128
GMLP_WIDTH = 2 * D_MODEL
GMLP_GROUPS = 8
GMLP_GROUP_DIM = GMLP_WIDTH // GMLP_GROUPS
RET_HEADS = 8
RET_DK = D_MODEL // RET_HEADS
RET_DV = 2 * RET_DK
RET_HK = RET_HEADS * RET_DK
RET_HV = RET_HEADS * RET_DV
RET_CHUNK = 128
ROT_BASE = 10000.0
D_FF = ((8 * D_MODEL // 3 + 127) // 128) * 128
CONV_W = 3
EPS = 1e-6

V7X_VMEM_BYTES = 64 * 1024 * 1024
VMEM_LIMIT_BYTES = V7X_VMEM_BYTES - 8 * 1024 * 1024
SUBLANES = 8
LANES = 128
MXU_DEPTH = 256
D_FF_PAD = -(-D_FF // (2 * MXU_DEPTH)) * (2 * MXU_DEPTH)
FFN_TN = 512
FFN_ROW_CHUNK = 256


def _params(*sem):
    return pltpu.CompilerParams(dimension_semantics=sem, vmem_limit_bytes=VMEM_LIMIT_BYTES)


def _rms(x, g):
    r = lax.rsqrt(jnp.mean(x * x, axis=-1, keepdims=True) + EPS)
    return x * r * g


def _group_norm(o):
    mu = jnp.mean(o, axis=-1, keepdims=True)
    d = o - mu
    var = jnp.mean(d * d, axis=-1, keepdims=True)
    return d * lax.rsqrt(var + EPS)


def _rotate(x, cos, sin_even, sin_odd):
    n = x.shape[-1]
    return x * cos + pltpu.roll(x, n - 1, 1) * sin_even + pltpu.roll(x, 1, 1) * sin_odd


def _rms_kernel(x_ref, g_ref, o_ref):
    o_ref[...] = _rms(x_ref[...], g_ref[...]).astype(o_ref.dtype)


def rmsnorm_rows(x, g, out_dtype):
    t, d = x.shape
    tm = min(256, t)
    return pl.pallas_call(
        _rms_kernel,
        out_shape=jax.ShapeDtypeStruct((t, d), out_dtype),
        grid=(t // tm,),
        in_specs=[pl.BlockSpec((tm, d), lambda i: (i, 0)), pl.BlockSpec((1, d), lambda i: (0, 0))],
        out_specs=pl.BlockSpec((tm, d), lambda i: (i, 0)),
        compiler_params=_params("parallel"),
        name="rmsnorm_rows",
    )(x, g.reshape(1, d))


def _mm_kernel(x_ref, w_ref, o_ref, w_s, *, act):
    @pl.when(pl.program_id(1) == 0)
    def _():
        w_s[...] = w_ref[...].astype(BF16)

    acc = jnp.dot(x_ref[...], w_s[...], preferred_element_type=F32)
    if act == "gelu":
        acc = jax.nn.gelu(acc)
    o_ref[...] = acc.astype(o_ref.dtype)


def matmul_act(x, w_all, layer, act, name):
    t, k = x.shape
    n = w_all.shape[2]
    tm = min(1024, t)
    tn = 1024
    return pl.pallas_call(
        functools.partial(_mm_kernel, act=act),
        out_shape=jax.ShapeDtypeStruct((t, n), F32),
        grid=(n // tn, t // tm),
        in_specs=[pl.BlockSpec((tm, k), lambda j, i: (i, 0)),
                  pl.BlockSpec((None, k, tn), lambda j, i: (layer, 0, j))],
        out_specs=pl.BlockSpec((tm, tn), lambda j, i: (i, j)),
        scratch_shapes=[pltpu.VMEM((k, tn), BF16)],
        compiler_params=_params("arbitrary", "arbitrary"),
        name=name,
    )(x, w_all)


def _layernorm(v, g, b):
    mu = jnp.mean(v, axis=-1, keepdims=True)
    d = v - mu
    var = jnp.mean(d * d, axis=-1, keepdims=True)
    return d * lax.rsqrt(var + EPS) * g + b


def _gmlp_gate_kernel(u_ref, v_ref, lng_ref, lnb_ref, ws_ref, bs_ref, x_ref):
    vn = _layernorm(v_ref[...], lng_ref[...], lnb_ref[...]).astype(BF16)
    c = GMLP_CHUNK
    causal = lax.broadcasted_iota(jnp.int32, (c, c), 0) >= lax.broadcasted_iota(jnp.int32, (c, c), 1)
    for g in range(GMLP_GROUPS):
        cols = slice(g * GMLP_GROUP_DIM, (g + 1) * GMLP_GROUP_DIM)
        w = jnp.where(causal, ws_ref[g], 0.0).astype(BF16)
        s = jnp.dot(w, vn[:, cols], preferred_element_type=F32) + bs_ref[g]
        x_ref[:, cols] = (u_ref[:, cols] * s).astype(x_ref.dtype)


def gmlp_gate_prompt(z, ln_g, ln_b, w_s, b_s):
    t = z.shape[0]
    w = GMLP_WIDTH
    c = GMLP_CHUNK
    bs_b = jnp.broadcast_to(b_s[:, :, None], (GMLP_GROUPS, c, GMLP_GROUP_DIM))
    return pl.pallas_call(
        _gmlp_gate_kernel,
        out_shape=jax.ShapeDtypeStruct((t, w), BF16),
        grid=(t // c,),
        in_specs=[
            pl.BlockSpec((c, w), lambda i: (i, 0)),
            pl.BlockSpec((c, w), lambda i: (i, 1)),
            pl.BlockSpec((1, w), lambda i: (0, 0)),
            pl.BlockSpec((1, w), lambda i: (0, 0)),
            pl.BlockSpec((GMLP_GROUPS, c, c), lambda i: (0, 0, 0)),
            pl.BlockSpec((GMLP_GROUPS, c, GMLP_GROUP_DIM), lambda i: (0, 0, 0)),
        ],
        out_specs=pl.BlockSpec((c, w), lambda i: (i, 0)),
        compiler_params=_params("parallel"),
        name="gmlp_gate_prompt",
    )(z, z, ln_g.reshape(1, w), ln_b.reshape(1, w), w_s, bs_b)


def _gmlp_gate_sample_kernel(u_ref, v_ref, lng_ref, lnb_ref, scale_ref, shift_ref, x_ref, vn_ref):
    vn = _layernorm(v_ref[...], lng_ref[...], lnb_ref[...])
    vn_ref[...] = vn
    x_ref[...] = (u_ref[...] * (vn * scale_ref[...] + shift_ref[...])).astype(x_ref.dtype)


def gmlp_gate_sample(z, ln_g, ln_b, w_s, b_s):
    t = z.shape[0]
    w = GMLP_WIDTH
    scale = jnp.repeat(w_s[:, 0, 0], GMLP_GROUP_DIM).reshape(1, w)
    shift = jnp.repeat(b_s[:, 0], GMLP_GROUP_DIM).reshape(1, w)
    row = pl.BlockSpec((1, w), lambda i: (0, 0))
    return pl.pallas_call(
        _gmlp_gate_sample_kernel,
        out_shape=(jax.ShapeDtypeStruct((t, w), BF16), jax.ShapeDtypeStruct((t, w), F32)),
        grid=(1,),
        in_specs=[pl.BlockSpec((t, w), lambda i: (0, 0)), pl.BlockSpec((t, w), lambda i: (0, 1)), row, row, row, row],
        out_specs=(pl.BlockSpec((t, w), lambda i: (0, 0)), pl.BlockSpec((t, w), lambda i: (0, 0))),
        compiler_params=_params("arbitrary"),
        name="gmlp_gate_sample",
    )(z, z, ln_g.reshape(1, w), ln_b.reshape(1, w), scale, shift)


def _rotary_tables(pos):
    angle = 1.0 / (ROT_BASE ** jnp.linspace(0.0, 1.0, RET_DK // 2, dtype=F32))
    ph = pos.astype(F32)[:, None] * angle[None, :]
    cos = jnp.cos(ph)
    sin = jnp.sin(ph)
    zero = jnp.zeros_like(sin)
    n = pos.shape[0]
    cos_r = jnp.stack([cos, cos], axis=-1).reshape(n, RET_DK)
    sin_even = jnp.stack([-sin, zero], axis=-1).reshape(n, RET_DK)
    sin_odd = jnp.stack([zero, sin], axis=-1).reshape(n, RET_DK)
    return cos_r, sin_even, sin_odd


def _log_gamma():
    return jnp.log(1.0 - 2.0 ** (-5.0 - jnp.arange(RET_HEADS, dtype=F32)))


def _ret_prompt_kernel(cdec_ref, q_ref, k_ref, v_ref, g_ref, cos_ref, se_ref, so_ref,
                       din_ref, qd_ref, kd_ref, x_ref, s_ref):
    @pl.when(pl.program_id(1) == 0)
    def _():
        s_ref[...] = jnp.zeros_like(s_ref)

    cos = cos_ref[...]
    se = se_ref[...]
    so = so_ref[...]
    for h in range(RET_HEADS):
        kc = slice(h * RET_DK, (h + 1) * RET_DK)
        vc = slice(h * RET_DV, (h + 1) * RET_DV)
        q = _rotate(q_ref[:, kc], cos, se, so)
        k = _rotate(k_ref[:, kc], cos, se, so) * (RET_DK ** -0.5)
        v = v_ref[:, vc].astype(BF16)
        scores = lax.dot_general(q.astype(BF16), k.astype(BF16), (((1,), (1,)), ((), ())),
                                 preferred_element_type=F32) * din_ref[h]
        s_old = s_ref[0, h]
        inner = jnp.dot(scores.astype(BF16), v, preferred_element_type=F32)
        cross = jnp.dot((q * qd_ref[h]).astype(BF16), s_old.astype(BF16), preferred_element_type=F32)
        kt = jnp.transpose(k * kd_ref[h]).astype(BF16)
        s_ref[0, h] = cdec_ref[h] * s_old + jnp.dot(kt, v, preferred_element_type=F32)
        on = _group_norm(inner + cross)
        x_ref[:, vc] = (jax.nn.silu(g_ref[:, vc]) * on).astype(x_ref.dtype)


def retention_prompt(qkvg, batch, seq):
    c = RET_CHUNK
    n = seq // c
    lg = _log_gamma()
    idx = jnp.arange(c, dtype=F32)
    diff = idx[:, None] - idx[None, :]
    decay_in = jnp.where(diff[None] >= 0, jnp.exp(lg[:, None, None] * jnp.maximum(diff, 0.0)[None]), 0.0)
    q_dec = jnp.exp(lg[:, None] * (idx + 1.0)[None, :])
    k_dec = jnp.exp(lg[:, None] * (c - 1.0 - idx)[None, :])
    chunk_dec = jnp.exp(lg * c)
    qd = jnp.broadcast_to(q_dec[:, :, None], (RET_HEADS, c, RET_DK))
    kd = jnp.broadcast_to(k_dec[:, :, None], (RET_HEADS, c, RET_DK))
    cos_r, sin_even, sin_odd = _rotary_tables(jnp.arange(seq, dtype=jnp.int32))

    rows = lambda b, j: b * n + j
    tab = pl.BlockSpec((c, RET_DK), lambda b, j: (j, 0))
    full3 = lambda shape: pl.BlockSpec(shape, lambda b, j: (0, 0, 0))
    return pl.pallas_call(
        _ret_prompt_kernel,
        out_shape=(jax.ShapeDtypeStruct((batch * seq, RET_HV), BF16),
                   jax.ShapeDtypeStruct((batch, RET_HEADS, RET_DK, RET_DV), F32)),
        grid=(batch, n),
        in_specs=[
            pl.BlockSpec(memory_space=pltpu.SMEM),
            pl.BlockSpec((c, RET_HK), lambda b, j: (rows(b, j), 0)),
            pl.BlockSpec((c, RET_HK), lambda b, j: (rows(b, j), 1)),
            pl.BlockSpec((c, RET_HV), lambda b, j: (rows(b, j), 1)),
            pl.BlockSpec((c, RET_HV), lambda b, j: (rows(b, j), 2)),
            tab, tab, tab,
            full3((RET_HEADS, c, c)), full3((RET_HEADS, c, RET_DK)), full3((RET_HEADS, c, RET_DK)),
        ],
        out_specs=(pl.BlockSpec((c, RET_HV), lambda b, j: (rows(b, j), 0)),
                   pl.BlockSpec((1, RET_HEADS, RET_DK, RET_DV), lambda b, j: (b, 0, 0, 0))),
        compiler_params=_params("parallel", "arbitrary"),
        name="retention_prompt",
    )(chunk_dec, qkvg, qkvg, qkvg, qkvg, cos_r, sin_even, sin_odd, decay_in, qd, kd)


def _ret_sample_kernel(cdec_ref, row_ref, cos_ref, se_ref, so_ref, gq_ref, s_ref, *rest):
    x_ref, so_out_ref = rest[-2:]
    row = row_ref[0]
    q = _rotate(row[:, 0:RET_HK], cos_ref[...], se_ref[...], so_ref[...])
    k = _rotate(row[:, RET_HK:2 * RET_HK], cos_ref[...], se_ref[...], so_ref[...]) * (RET_DK ** -0.5)
    v = row[:, 2 * RET_HK:2 * RET_HK + RET_HV]
    g = row[:, 2 * RET_HK + RET_HV:]
    qk = q * k

    pad = LANES
    head_of_row = lax.broadcasted_iota(jnp.int32, (pad, RET_HK), 0)
    head_of_col = lax.shift_right_logical(lax.broadcasted_iota(jnp.int32, (pad, RET_HK), 1),
                                          int(math.log2(RET_DK)))
    on_diag = head_of_row == head_of_col
    q_bd = jnp.where(on_diag[:SUBLANES], q * gq_ref[...], 0.0).astype(BF16)
    k_bd = jnp.where(on_diag, k, 0.0)
    s_old = s_ref[0]
    cross = jnp.dot(q_bd, s_old.astype(BF16), preferred_element_type=F32)
    v_row_id = lax.broadcasted_iota(jnp.int32, (pad, RET_DV), 0)
    v_rows = jnp.zeros((pad, RET_DV), F32)
    for h in range(RET_HEADS):
        v_rows = jnp.where(v_row_id == h, v[:, h * RET_DV:(h + 1) * RET_DV], v_rows)
    outer = jnp.dot(jnp.transpose(k_bd).astype(BF16), v_rows.astype(BF16), preferred_element_type=F32)
    for h in range(RET_HEADS):
        kc = slice(h * RET_DK, (h + 1) * RET_DK)
        vc = slice(h * RET_DV, (h + 1) * RET_DV)
        score = jnp.sum(qk[:, kc], axis=-1, keepdims=True)
        on = _group_norm(score * v[:, vc] + cross[h:h + 1, :])
        x_ref[0, :, vc] = (jax.nn.silu(g[:, vc]) * on).astype(x_ref.dtype)
        so_out_ref[0, kc, :] = cdec_ref[h] * s_old[kc, :] + outer[kc, :]


def retention_sample(qkvg, state_all, li, new_states):
    b = qkvg.shape[0]
    n_in = qkvg.shape[1]
    n_layers = state_all.shape[0]
    lg = _log_gamma()
    gamma = jnp.exp(lg * 1.0)
    gq = jnp.repeat(gamma, RET_DK).reshape(1, RET_HK)
    cos_r, sin_even, sin_odd = _rotary_tables(PAST_LEN + jnp.arange(1, dtype=jnp.int32))
    tile = lambda t: jnp.tile(t, (1, RET_HEADS))
    vec = pl.BlockSpec((1, RET_HK), lambda i: (0, 0))
    state_spec = pl.BlockSpec((None, 1, RET_HK, RET_DV), lambda i: (li, i, 0, 0))
    in_specs = [
        pl.BlockSpec(memory_space=pltpu.SMEM),
        pl.BlockSpec((1, 1, n_in), lambda i: (i, 0, 0)),
        vec, vec, vec, vec,
        state_spec,
    ]
    args = [gamma, qkvg.reshape(b, 1, n_in), tile(cos_r), tile(sin_even), tile(sin_odd), gq,
            state_all.reshape(n_layers, b, RET_HK, RET_DV)]
    aliases = {}
    if new_states is not None:
        aliases = {len(args): 1}
        in_specs.append(pl.BlockSpec(memory_space=pl.ANY))
        args.append(new_states)
    x, s_new = pl.pallas_call(
        _ret_sample_kernel,
        out_shape=(jax.ShapeDtypeStruct((b, 1, RET_HV), BF16),
                   jax.ShapeDtypeStruct((n_layers, b, RET_HK, RET_DV), F32)),
        grid=(b,),
        in_specs=in_specs,
        out_specs=(pl.BlockSpec((1, 1, RET_HV), lambda i: (i, 0, 0)), state_spec),
        input_output_aliases=aliases,
        compiler_params=_params("parallel"),
        name="retention_sample",
    )(*args)
    return x.reshape(b, RET_HV), s_new


PROJ_W_CHUNKS = 8


def _proj_res_kernel(x_ref, w_ref, h_ref, g_ref, ho_ref, hn_ref, w_s, *, k_rows):
    step = pl.program_id(0)
    chunk = w_ref.shape[0]

    @pl.when(step < PROJ_W_CHUNKS)
    def _():
        if w_s.shape[0] > k_rows:
            @pl.when(step == 0)
            def _():
                w_s[k_rows:, :] = jnp.zeros((w_s.shape[0] - k_rows, w_s.shape[1]), BF16)

        start = pl.multiple_of(step * chunk, 2 * SUBLANES)
        w_s[pl.ds(start, chunk), :] = w_ref[...].astype(BF16)

    @pl.when(step >= PROJ_W_CHUNKS)
    def _():
        h_new = h_ref[...] + jnp.dot(x_ref[...], w_s[...], preferred_element_type=F32)
        ho_ref[...] = h_new
        hn_ref[...] = _rms(h_new, g_ref[...]).astype(hn_ref.dtype)


def proj_residual(x, w_all, layer, h, g_next, name):
    t, k = x.shape
    k_rows, d = w_all.shape[1], w_all.shape[2]
    chunk = k_rows // PROJ_W_CHUNKS
    assert chunk * PROJ_W_CHUNKS == k_rows and chunk % (2 * SUBLANES) == 0
    tm = min(256, t)
    blk = lambda s: (jnp.maximum(s - PROJ_W_CHUNKS, 0), 0)
    return pl.pallas_call(
        functools.partial(_proj_res_kernel, k_rows=k_rows),
        out_shape=(jax.ShapeDtypeStruct((t, d), F32), jax.ShapeDtypeStruct((t, d), BF16)),
        grid=(PROJ_W_CHUNKS + t // tm,),
        in_specs=[
            pl.BlockSpec((tm, k), blk),
            pl.BlockSpec((None, chunk, d), lambda s: (layer, jnp.minimum(s, PROJ_W_CHUNKS - 1), 0)),
            pl.BlockSpec((tm, d), blk),
            pl.BlockSpec((1, d), lambda s: (0, 0)),
        ],
        out_specs=(pl.BlockSpec((tm, d), blk), pl.BlockSpec((tm, d), blk)),
        scratch_shapes=[pltpu.VMEM((k, d), BF16)],
        compiler_params=_params("arbitrary"),
        name=name,
    )(x, w_all, h, g_next.reshape(1, d))


def _conv_gate(a, a1, a2, b, cw_ref, cb_ref):
    conv = cb_ref[...] + cw_ref[0:1, :] * a2 + cw_ref[1:2, :] * a1 + cw_ref[2:3, :] * a
    return jax.nn.gelu(conv) * b


FFN_W_BLOCKS = FFN_TN // LANES
FFN_COL_BLOCKS = D_FF // LANES


def _ffn_weight_specs(layer, d, col_tile_of):
    last = 2 * FFN_COL_BLOCKS - 1

    def spec(first_block, r):
        def index(*grid_idx):
            return (layer, 0, jnp.minimum(first_block + FFN_W_BLOCKS * col_tile_of(*grid_idx) + r, last))
        return pl.BlockSpec((None, d, LANES), index)

    return ([spec(0, r) for r in range(FFN_W_BLOCKS)]
            + [spec(FFN_COL_BLOCKS, r) for r in range(FFN_W_BLOCKS)])


def _cast_weight_blocks(block_refs, dst):
    for r, ref in enumerate(block_refs):
        dst[:, r * LANES:(r + 1) * LANES] = ref[...].astype(BF16)


def _valid_cols(col_tile, shape):
    return col_tile * FFN_TN + lax.broadcasted_iota(jnp.int32, shape, 1) < D_FF


def _ffn_in_prompt_kernel(x_ref, *refs):
    wa_refs, wb_refs = refs[:FFN_W_BLOCKS], refs[FFN_W_BLOCKS:2 * FFN_W_BLOCKS]
    cw_ref, cb_ref, y_ref, st_ref, wa_s, wb_s = refs[2 * FFN_W_BLOCKS:]

    @pl.when(pl.program_id(1) == 0)
    def _():
        _cast_weight_blocks(wa_refs, wa_s)
        _cast_weight_blocks(wb_refs, wb_s)

    seq = x_ref.shape[0]
    rc = FFN_ROW_CHUNK
    tn = FFN_TN
    rows = lax.broadcasted_iota(jnp.int32, (rc, tn), 0)
    valid = _valid_cols(pl.program_id(0), (rc, tn))
    prev1 = jnp.zeros((1, tn), F32)
    prev2 = jnp.zeros((1, tn), F32)
    for r in range(seq // rc):
        xs = x_ref[r * rc:(r + 1) * rc, :]
        a = jnp.dot(xs, wa_s[...], preferred_element_type=F32)
        b = jnp.dot(xs, wb_s[...], preferred_element_type=F32)
        a1 = jnp.where(rows >= 1, pltpu.roll(a, 1, 0), prev1)
        a2 = jnp.where(rows >= 2, pltpu.roll(a, 2, 0), jnp.where(rows == 1, prev1, prev2))
        y = jnp.where(valid, _conv_gate(a, a1, a2, b, cw_ref, cb_ref), 0.0)
        y_ref[r * rc:(r + 1) * rc, :] = y.astype(y_ref.dtype)
        prev1 = a[rc - 1:rc, :]
        prev2 = a[rc - 2:rc - 1, :]
        if r == seq // rc - 1:
            st_ref[0] = a[rc - SUBLANES:, :]


def ffn_in_prompt(hn, w_in_all, layer, cw, cb, batch, seq):
    tn = FFN_TN
    d = hn.shape[1]
    y, st = pl.pallas_call(
        _ffn_in_prompt_kernel,
        out_shape=(jax.ShapeDtypeStruct((batch * seq, D_FF_PAD), BF16),
                   jax.ShapeDtypeStruct((batch, SUBLANES, D_FF_PAD), F32)),
        grid=(D_FF_PAD // tn, batch),
        in_specs=[pl.BlockSpec((seq, d), lambda j, b: (b, 0))]
        + _ffn_weight_specs(layer, d, lambda j, b: j)
        + [pl.BlockSpec((CONV_W, tn), lambda j, b: (0, j)), pl.BlockSpec((1, tn), lambda j, b: (0, j))],
        out_specs=(pl.BlockSpec((seq, tn), lambda j, b: (b, j)),
                   pl.BlockSpec((1, SUBLANES, tn), lambda j, b: (b, 0, j))),
        scratch_shapes=[pltpu.VMEM((d, tn), BF16), pltpu.VMEM((d, tn), BF16)],
        compiler_params=_params("arbitrary", "arbitrary"),
        name="ffn_in_prompt",
    )(hn, *([w_in_all] * (2 * FFN_W_BLOCKS)), cw, cb)
    return y, st[:, SUBLANES - (CONV_W - 1):, :D_FF]


def _ffn_in_sample_kernel(x_ref, *refs):
    wa_refs, wb_refs = refs[:FFN_W_BLOCKS], refs[FFN_W_BLOCKS:2 * FFN_W_BLOCKS]
    cw_ref, cb_ref, b0_ref, b1_ref, y_ref, a_ref, wa_s, wb_s = refs[2 * FFN_W_BLOCKS:]
    _cast_weight_blocks(wa_refs, wa_s)
    _cast_weight_blocks(wb_refs, wb_s)
    x = x_ref[...]
    a = jnp.dot(x, wa_s[...], preferred_element_type=F32)
    b = jnp.dot(x, wb_s[...], preferred_element_type=F32)
    a_ref[...] = a
    y = jnp.where(_valid_cols(pl.program_id(0), a.shape),
                  _conv_gate(a, b1_ref[...], b0_ref[...], b, cw_ref, cb_ref), 0.0)
    y_ref[...] = y.astype(y_ref.dtype)


def ffn_in_sample(hn, w_in_all, layer, cw, cb, buf):
    t, d = hn.shape
    tn = FFN_TN
    pad = ((0, 0), (0, D_FF_PAD - D_FF))
    b0 = jnp.pad(buf[:, 0, :], pad)
    b1 = jnp.pad(buf[:, 1, :], pad)
    col = lambda rows_: pl.BlockSpec((rows_, tn), lambda j: (0, j))
    y, a = pl.pallas_call(
        _ffn_in_sample_kernel,
        out_shape=(jax.ShapeDtypeStruct((t, D_FF_PAD), BF16), jax.ShapeDtypeStruct((t, D_FF_PAD), F32)),
        grid=(D_FF_PAD // tn,),
        in_specs=[pl.BlockSpec((t, d), lambda j: (0, 0))]
        + _ffn_weight_specs(layer, d, lambda j: j)
        + [col(CONV_W), col(1), col(t), col(t)],
        out_specs=(col(t), col(t)),
        scratch_shapes=[pltpu.VMEM((d, tn), BF16), pltpu.VMEM((d, tn), BF16)],
        compiler_params=_params("arbitrary"),
        name="ffn_in_sample",
    )(hn, *([w_in_all] * (2 * FFN_W_BLOCKS)), cw, cb, b0, b1)
    return y, jnp.stack([buf[:, 1, :], a[:, :D_FF]], axis=1)


def _ple_kernel(hn_ref, wg_ref, p_ref, wp_ref, h_ref, g_ref, *refs, emit_h):
    out_refs, (wg_s, wp_s) = refs[:-2], refs[-2:]

    @pl.when(pl.program_id(0) == 0)
    def _():
        wg_s[...] = wg_ref[...].astype(BF16)
        wp_s[...] = wp_ref[...].astype(BF16)

    gate = jax.nn.sigmoid(jnp.dot(hn_ref[...], wg_s[...], preferred_element_type=F32))
    proj = jnp.dot(p_ref[...].astype(BF16), wp_s[...], preferred_element_type=F32)
    h_new = h_ref[...] + gate * proj
    if emit_h:
        out_refs[0][...] = h_new
    out_refs[-1][...] = _rms(h_new, g_ref[...]).astype(out_refs[-1].dtype)


def ple_residual(hn, wg_all, p_all, layer, wp_all, h, g_next, emit_h):
    t, d = h.shape
    pd = p_all.shape[-1]
    p_all = p_all.reshape(p_all.shape[0], t, pd)
    tm = min(256, t)
    rows = lambda width: pl.BlockSpec((tm, width), lambda i: (i, 0))
    const = lambda r, c: pl.BlockSpec((None, r, c), lambda i: (layer, 0, 0), pipeline_mode=pl.Buffered(1))
    if emit_h:
        out_shape = (jax.ShapeDtypeStruct((t, d), F32), jax.ShapeDtypeStruct((t, d), BF16))
        out_specs = (rows(d), rows(d))
    else:
        out_shape = (jax.ShapeDtypeStruct((t, d), F32),)
        out_specs = (rows(d),)
    return pl.pallas_call(
        functools.partial(_ple_kernel, emit_h=emit_h),
        out_shape=out_shape,
        grid=(t // tm,),
        in_specs=[rows(d), const(d, d), pl.BlockSpec((None, tm, pd), lambda i: (layer, i, 0)), const(pd, d),
                  rows(d), pl.BlockSpec((1, d), lambda i: (0, 0))],
        out_specs=out_specs,
        scratch_shapes=[pltpu.VMEM((d, d), BF16), pltpu.VMEM((pd, d), BF16)],
        compiler_params=_params("arbitrary"),
        name="ple_residual",
    )(hn, wg_all, p_all, wp_all, h, g_next.reshape(1, d))


def _trunk(x, p, batch, seq, ret_state0, conv_state0, w, is_sample):
    t = batch * seq
    h = x.reshape(t, D_MODEL)
    hn = rmsnorm_rows(h, w["norm_mix_g"][0], BF16)
    ret_new, conv_new, v_new = [], [], []
    ret_stacked = None
    y_final = None
    for i in range(DEPTH):
        li = i // N_MIXERS
        if i % N_MIXERS == 0:
            z = matmul_act(hn, w["gmlp_w_in"], li, "gelu", "gmlp_in")
            if is_sample:
                xm, vn = gmlp_gate_sample(z, w["gmlp_ln_g"][li], w["gmlp_ln_b"][li], w["gmlp_w_s"][li], w["gmlp_b_s"][li])
                v_new.append(vn.reshape(batch, seq, GMLP_WIDTH))
            else:
                xm = gmlp_gate_prompt(z, w["gmlp_ln_g"][li], w["gmlp_ln_b"][li], w["gmlp_w_s"][li], w["gmlp_b_s"][li])
            w_out = w["gmlp_w_out"]
        else:
            qkvg = matmul_act(hn, w["ret_w_in"], li, None, "ret_in")
            if is_sample:
                xm, ret_stacked = retention_sample(qkvg, ret_state0, li, ret_stacked)
            else:
                xm, s_new = retention_prompt(qkvg, batch, seq)
                ret_new.append(s_new)
            w_out = w["ret_w_out"]
        h, hn = proj_residual(xm, w_out, li, h, w["norm_ffn_g"][i], "mixer_out")
        if is_sample:
            y, buf = ffn_in_sample(hn, w["ffn_w_in"], i, w["ffn_cw"][i], w["ffn_cb"][i], conv_state0[i])
        else:
            y, buf = ffn_in_prompt(hn, w["ffn_w_in"], i, w["ffn_cw"][i], w["ffn_cb"][i], batch, seq)
        conv_new.append(buf)
        h, hn = proj_residual(y, w["ffn_w_down"], i, h, w["norm_ple_g"][i], "ffn_down")
        last = i == DEPTH - 1
        g_next = w["final_norm_g"] if last else w["norm_mix_g"][i + 1]
        outs = ple_residual(hn, w["ple_w_gate"], p, i, w["ple_w_proj"], h, g_next, not last)
        if last:
            y_final = outs[0]
        else:
            h, hn = outs
    v_out = jnp.stack(v_new) if is_sample else None
    if is_sample:
        ret_out = ret_stacked.reshape(ret_state0.shape)
    else:
        ret_out = jnp.stack(ret_new)
    return y_final.reshape(batch, seq, D_MODEL), ret_out, jnp.stack(conv_new), v_out


def kernel(x_prompt, x_sample, state_ret, state_conv, p_prompt, p_sample, norm_mix_g, norm_ffn_g, norm_ple_g, final_norm_g, gmlp_w_in, gmlp_ln_g, gmlp_ln_b, gmlp_w_s, gmlp_b_s, gmlp_w_out, ret_w_in, ret_w_out, ffn_w_in, ffn_conv_w, ffn_conv_b, ffn_w_down, ple_w_proj, ple_w_gate):
    ff_pad = D_FF_PAD - D_FF
    w = {
        "norm_mix_g": norm_mix_g, "norm_ffn_g": norm_ffn_g, "norm_ple_g": norm_ple_g, "final_norm_g": final_norm_g,
        "gmlp_w_in": gmlp_w_in, "gmlp_ln_g": gmlp_ln_g, "gmlp_ln_b": gmlp_ln_b,
        "gmlp_w_s": gmlp_w_s, "gmlp_b_s": gmlp_b_s, "gmlp_w_out": gmlp_w_out,
        "ret_w_in": ret_w_in, "ret_w_out": ret_w_out,
        "ffn_w_in": ffn_w_in,
        "ffn_cw": jnp.pad(ffn_conv_w, ((0, 0), (0, 0), (0, ff_pad))),
        "ffn_cb": jnp.pad(ffn_conv_b, ((0, 0), (0, ff_pad))).reshape(DEPTH, 1, D_FF_PAD),
        "ffn_w_down": ffn_w_down,
        "ple_w_proj": ple_w_proj, "ple_w_gate": ple_w_gate,
    }
    bp, lp = x_prompt.shape[0], x_prompt.shape[1]
    bs, ls = x_sample.shape[0], x_sample.shape[1]
    y_prompt, ret_prompt, conv_prompt, _ = _trunk(x_prompt, p_prompt, bp, lp, None, None, w, False)
    y_sample, ret_sample, conv_sample, v_sample = _trunk(x_sample, p_sample, bs, ls, state_ret, state_conv, w, True)
    return (y_prompt, y_sample, ret_prompt, ret_sample, conv_prompt, conv_sample, v_sample)
```

```python
import functools
import math

import jax
import jax.numpy as jnp
from jax import lax
from jax.experimental import pallas as pl
from jax.experimental.pallas import tpu as pltpu

F32 = jnp.float32
BF16 = jnp.bfloat16

D_MODEL = 2048
DEPTH = 4
PAST_LEN = 16384
N_MIXERS = 2
PLE_DIM = 256
GMLP_CHUNK = 128
GMLP_WIDTH = 2 * D_MODEL
GMLP_GROUPS = 8
GMLP_GROUP_DIM = GMLP_WIDTH // GMLP_GROUPS
RET_HEADS = 8
RET_DK = D_MODEL // RET_HEADS
RET_DV = 2 * RET_DK
RET_HK = RET_HEADS * RET_DK
RET_HV = RET_HEADS * RET_DV
RET_CHUNK = 128
ROT_BASE = 10000.0
D_FF = ((8 * D_MODEL // 3 + 127) // 128) * 128
CONV_W = 3
EPS = 1e-6

V7X_VMEM_BYTES = 64 * 1024 * 1024
VMEM_LIMIT_BYTES = V7X_VMEM_BYTES - 8 * 1024 * 1024
SUBLANES = 8
LANES = 128
MXU_DEPTH = 256
D_FF_PAD = -(-D_FF // (2 * MXU_DEPTH)) * (2 * MXU_DEPTH)
FFN_TN = 512
FFN_ROW_CHUNK = 256
MM_TM = 1024
MM_TN = 1024
PROJ_TM = 256
PLE_TM = 256
MIXER_OUT_W_CHUNK = 512
FFN_DOWN_W_CHUNK = 128


def _params(*sem):
    return pltpu.CompilerParams(dimension_semantics=sem, vmem_limit_bytes=VMEM_LIMIT_BYTES)


def _rms(x, g):
    r = lax.rsqrt(jnp.mean(x * x, axis=-1, keepdims=True) + EPS)
    return x * r * g


def _group_norm(o):
    mu = jnp.mean(o, axis=-1, keepdims=True)
    d = o - mu
    var = jnp.mean(d * d, axis=-1, keepdims=True)
    return d * lax.rsqrt(var + EPS)


def _rotate(x, cos, sin_even, sin_odd):
    n = x.shape[-1]
    return x * cos + pltpu.roll(x, n - 1, 1) * sin_even + pltpu.roll(x, 1, 1) * sin_odd


def _rms_kernel(x_ref, g_ref, o_ref):
    o_ref[...] = _rms(x_ref[...], g_ref[...]).astype(o_ref.dtype)


def rmsnorm_rows(x, g, out_dtype):
    t, d = x.shape
    tm = min(256, t)
    return pl.pallas_call(
        _rms_kernel,
        out_shape=jax.ShapeDtypeStruct((t, d), out_dtype),
        grid=(t // tm,),
        in_specs=[pl.BlockSpec((tm, d), lambda i: (i, 0)), pl.BlockSpec((1, d), lambda i: (0, 0))],
        out_specs=pl.BlockSpec((tm, d), lambda i: (i, 0)),
        compiler_params=_params("parallel"),
        name="rmsnorm_rows",
    )(x, g.reshape(1, d))


def _mm_kernel(xp_ref, xs_ref, w_ref, op_ref, os_ref, w_s, *, act, n_blocks):
    i = pl.program_id(1)

    @pl.when(i == 0)
    def _():
        w_s[...] = w_ref[...].astype(BF16)

    def run(x_ref, o_ref):
        acc = jnp.dot(x_ref[...], w_s[...], preferred_element_type=F32)
        if act == "gelu":
            acc = jax.nn.gelu(acc)
        o_ref[...] = acc.astype(o_ref.dtype)

    @pl.when(i < n_blocks)
    def _():
        run(xp_ref, op_ref)

    @pl.when(i == n_blocks)
    def _():
        run(xs_ref, os_ref)


def matmul_act(xp, xs, w_all, layer, act, name):
    tp, k = xp.shape
    ts = xs.shape[0]
    n = w_all.shape[2]
    tm = MM_TM
    tn = MM_TN
    nb = tp // tm
    blk = lambda i: jnp.minimum(i, nb - 1)
    return pl.pallas_call(
        functools.partial(_mm_kernel, act=act, n_blocks=nb),
        out_shape=(jax.ShapeDtypeStruct((tp, n), F32), jax.ShapeDtypeStruct((ts, n), F32)),
        grid=(n // tn, nb + 1),
        in_specs=[pl.BlockSpec((tm, k), lambda j, i: (blk(i), 0)),
                  pl.BlockSpec((ts, k), lambda j, i: (0, 0)),
                  pl.BlockSpec((None, k, tn), lambda j, i: (layer, 0, j))],
        out_specs=(pl.BlockSpec((tm, tn), lambda j, i: (blk(i), j)),
                   pl.BlockSpec((ts, tn), lambda j, i: (0, j))),
        scratch_shapes=[pltpu.VMEM((k, tn), BF16)],
        compiler_params=_params("arbitrary", "arbitrary"),
        name=name,
    )(xp, xs, w_all)


def _layernorm(v, g, b):
    mu = jnp.mean(v, axis=-1, keepdims=True)
    d = v - mu
    var = jnp.mean(d * d, axis=-1, keepdims=True)
    return d * lax.rsqrt(var + EPS) * g + b


def _gmlp_gate_kernel(u_ref, v_ref, lng_ref, lnb_ref, ws_ref, bs_ref, x_ref):
    vn = _layernorm(v_ref[...], lng_ref[...], lnb_ref[...]).astype(BF16)
    c = GMLP_CHUNK
    causal = lax.broadcasted_iota(jnp.int32, (c, c), 0) >= lax.broadcasted_iota(jnp.int32, (c, c), 1)
    for g in range(GMLP_GROUPS):
        cols = slice(g * GMLP_GROUP_DIM, (g + 1) * GMLP_GROUP_DIM)
        w = jnp.where(causal, ws_ref[g], 0.0).astype(BF16)
        s = jnp.dot(w, vn[:, cols], preferred_element_type=F32) + bs_ref[g]
        x_ref[:, cols] = (u_ref[:, cols] * s).astype(x_ref.dtype)


def gmlp_gate_prompt(z, ln_g, ln_b, w_s, b_s):
    t = z.shape[0]
    w = GMLP_WIDTH
    c = GMLP_CHUNK
    bs_b = jnp.broadcast_to(b_s[:, :, None], (GMLP_GROUPS, c, GMLP_GROUP_DIM))
    return pl.pallas_call(
        _gmlp_gate_kernel,
        out_shape=jax.ShapeDtypeStruct((t, w), BF16),
        grid=(t // c,),
        in_specs=[
            pl.BlockSpec((c, w), lambda i: (i, 0)),
            pl.BlockSpec((c, w), lambda i: (i, 1)),
            pl.BlockSpec((1, w), lambda i: (0, 0)),
            pl.BlockSpec((1, w), lambda i: (0, 0)),
            pl.BlockSpec((GMLP_GROUPS, c, c), lambda i: (0, 0, 0)),
            pl.BlockSpec((GMLP_GROUPS, c, GMLP_GROUP_DIM), lambda i: (0, 0, 0)),
        ],
        out_specs=pl.BlockSpec((c, w), lambda i: (i, 0)),
        compiler_params=_params("parallel"),
        name="gmlp_gate_prompt",
    )(z, z, ln_g.reshape(1, w), ln_b.reshape(1, w), w_s, bs_b)


def _gmlp_gate_sample_kernel(u_ref, v_ref, lng_ref, lnb_ref, scale_ref, shift_ref, x_ref, vn_ref):
    vn = _layernorm(v_ref[...], lng_ref[...], lnb_ref[...])
    vn_ref[...] = vn
    x_ref[...] = (u_ref[...] * (vn * scale_ref[...] + shift_ref[...])).astype(x_ref.dtype)


def gmlp_gate_sample(z, ln_g, ln_b, w_s, b_s):
    t = z.shape[0]
    w = GMLP_WIDTH
    scale = jnp.repeat(w_s[:, 0, 0], GMLP_GROUP_DIM).reshape(1, w)
    shift = jnp.repeat(b_s[:, 0], GMLP_GROUP_DIM).reshape(1, w)
    row = pl.BlockSpec((1, w), lambda i: (0, 0))
    return pl.pallas_call(
        _gmlp_gate_sample_kernel,
        out_shape=(jax.ShapeDtypeStruct((t, w), BF16), jax.ShapeDtypeStruct((t, w), F32)),
        grid=(1,),
        in_specs=[pl.BlockSpec((t, w), lambda i: (0, 0)), pl.BlockSpec((t, w), lambda i: (0, 1)), row, row, row, row],
        out_specs=(pl.BlockSpec((t, w), lambda i: (0, 0)), pl.BlockSpec((t, w), lambda i: (0, 0))),
        compiler_params=_params("arbitrary"),
        name="gmlp_gate_sample",
    )(z, z, ln_g.reshape(1, w), ln_b.reshape(1, w), scale, shift)


def _rotary_tables(pos):
    angle = 1.0 / (ROT_BASE ** jnp.linspace(0.0, 1.0, RET_DK // 2, dtype=F32))
    ph = pos.astype(F32)[:, None] * angle[None, :]
    cos = jnp.cos(ph)
    sin = jnp.sin(ph)
    zero = jnp.zeros_like(sin)
    n = pos.shape[0]
    cos_r = jnp.stack([cos, cos], axis=-1).reshape(n, RET_DK)
    sin_even = jnp.stack([-sin, zero], axis=-1).reshape(n, RET_DK)
    sin_odd = jnp.stack([zero, sin], axis=-1).reshape(n, RET_DK)
    return cos_r, sin_even, sin_odd


def _log_gamma():
    return jnp.log(1.0 - 2.0 ** (-5.0 - jnp.arange(RET_HEADS, dtype=F32)))


def _ret_prompt_kernel(cdec_ref, q_ref, k_ref, v_ref, g_ref, cos_ref, se_ref, so_ref,
                       din_ref, qd_ref, kd_ref, x_ref, s_ref):
    @pl.when(pl.program_id(1) == 0)
    def _():
        s_ref[...] = jnp.zeros_like(s_ref)

    cos = cos_ref[...]
    se = se_ref[...]
    so = so_ref[...]
    for h in range(RET_HEADS):
        kc = slice(h * RET_DK, (h + 1) * RET_DK)
        vc = slice(h * RET_DV, (h + 1) * RET_DV)
        q = _rotate(q_ref[:, kc], cos, se, so)
        k = _rotate(k_ref[:, kc], cos, se, so) * (RET_DK ** -0.5)
        v = v_ref[:, vc].astype(BF16)
        scores = lax.dot_general(q.astype(BF16), k.astype(BF16), (((1,), (1,)), ((), ())),
                                 preferred_element_type=F32) * din_ref[h]
        s_old = s_ref[0, h]
        lhs = jnp.concatenate([scores.astype(BF16), (q * qd_ref[h]).astype(BF16)], axis=1)
        rhs = jnp.concatenate([v, s_old.astype(BF16)], axis=0)
        o = jnp.dot(lhs, rhs, preferred_element_type=F32)
        kt = jnp.transpose(k * kd_ref[h]).astype(BF16)
        s_ref[0, h] = cdec_ref[h] * s_old + jnp.dot(kt, v, preferred_element_type=F32)
        on = _group_norm(o)
        x_ref[:, vc] = (jax.nn.silu(g_ref[:, vc]) * on).astype(x_ref.dtype)


def retention_prompt(qkvg, batch, seq):
    c = RET_CHUNK
    n = seq // c
    lg = _log_gamma()
    idx = jnp.arange(c, dtype=F32)
    diff = idx[:, None] - idx[None, :]
    decay_in = jnp.where(diff[None] >= 0, jnp.exp(lg[:, None, None] * jnp.maximum(diff, 0.0)[None]), 0.0)
    q_dec = jnp.exp(lg[:, None] * (idx + 1.0)[None, :])
    k_dec = jnp.exp(lg[:, None] * (c - 1.0 - idx)[None, :])
    chunk_dec = jnp.exp(lg * c)
    qd = jnp.broadcast_to(q_dec[:, :, None], (RET_HEADS, c, RET_DK))
    kd = jnp.broadcast_to(k_dec[:, :, None], (RET_HEADS, c, RET_DK))
    cos_r, sin_even, sin_odd = _rotary_tables(jnp.arange(seq, dtype=jnp.int32))

    rows = lambda b, j: b * n + j
    tab = pl.BlockSpec((c, RET_DK), lambda b, j: (j, 0))
    full3 = lambda shape: pl.BlockSpec(shape, lambda b, j: (0, 0, 0))
    return pl.pallas_call(
        _ret_prompt_kernel,
        out_shape=(jax.ShapeDtypeStruct((batch * seq, RET_HV), BF16),
                   jax.ShapeDtypeStruct((batch, RET_HEADS, RET_DK, RET_DV), F32)),
        grid=(batch, n),
        in_specs=[
            pl.BlockSpec(memory_space=pltpu.SMEM),
            pl.BlockSpec((c, RET_HK), lambda b, j: (rows(b, j), 0)),
            pl.BlockSpec((c, RET_HK), lambda b, j: (rows(b, j), 1)),
            pl.BlockSpec((c, RET_HV), lambda b, j: (rows(b, j), 1)),
            pl.BlockSpec((c, RET_HV), lambda b, j: (rows(b, j), 2)),
            tab, tab, tab,
            full3((RET_HEADS, c, c)), full3((RET_HEADS, c, RET_DK)), full3((RET_HEADS, c, RET_DK)),
        ],
        out_specs=(pl.BlockSpec((c, RET_HV), lambda b, j: (rows(b, j), 0)),
                   pl.BlockSpec((1, RET_HEADS, RET_DK, RET_DV), lambda b, j: (b, 0, 0, 0))),
        compiler_params=_params("parallel", "arbitrary"),
        name="retention_prompt",
    )(chunk_dec, qkvg, qkvg, qkvg, qkvg, cos_r, sin_even, sin_odd, decay_in, qd, kd)


def _ret_sample_kernel(cdec_ref, row_ref, cos_ref, se_ref, so_ref, gq_ref, s_ref, *rest):
    x_ref, so_out_ref = rest[-2:]
    row = row_ref[0]
    q = _rotate(row[:, 0:RET_HK], cos_ref[...], se_ref[...], so_ref[...])
    k = _rotate(row[:, RET_HK:2 * RET_HK], cos_ref[...], se_ref[...], so_ref[...]) * (RET_DK ** -0.5)
    v = row[:, 2 * RET_HK:2 * RET_HK + RET_HV]
    g = row[:, 2 * RET_HK + RET_HV:]
    qk = q * k

    pad = LANES
    head_of_row = lax.broadcasted_iota(jnp.int32, (pad, RET_HK), 0)
    head_of_col = lax.shift_right_logical(lax.broadcasted_iota(jnp.int32, (pad, RET_HK), 1),
                                          int(math.log2(RET_DK)))
    on_diag = head_of_row == head_of_col
    q_bd = jnp.where(on_diag[:SUBLANES], q * gq_ref[...], 0.0).astype(BF16)
    k_bd = jnp.where(on_diag, k, 0.0)
    s_old = s_ref[0]
    cross = jnp.dot(q_bd, s_old.astype(BF16), preferred_element_type=F32)
    v_row_id = lax.broadcasted_iota(jnp.int32, (pad, RET_DV), 0)
    v_rows = jnp.zeros((pad, RET_DV), F32)
    for h in range(RET_HEADS):
        v_rows = jnp.where(v_row_id == h, v[:, h * RET_DV:(h + 1) * RET_DV], v_rows)
    outer = jnp.dot(jnp.transpose(k_bd).astype(BF16), v_rows.astype(BF16), preferred_element_type=F32)
    for h in range(RET_HEADS):
        kc = slice(h * RET_DK, (h + 1) * RET_DK)
        vc = slice(h * RET_DV, (h + 1) * RET_DV)
        score = jnp.sum(qk[:, kc], axis=-1, keepdims=True)
        on = _group_norm(score * v[:, vc] + cross[h:h + 1, :])
        x_ref[0, :, vc] = (jax.nn.silu(g[:, vc]) * on).astype(x_ref.dtype)
        so_out_ref[0, kc, :] = cdec_ref[h] * s_old[kc, :] + outer[kc, :]


def retention_sample(qkvg, state_all, li, new_states):
    b = qkvg.shape[0]
    n_in = qkvg.shape[1]
    n_layers = state_all.shape[0]
    lg = _log_gamma()
    gamma = jnp.exp(lg * 1.0)
    gq = jnp.repeat(gamma, RET_DK).reshape(1, RET_HK)
    cos_r, sin_even, sin_odd = _rotary_tables(PAST_LEN + jnp.arange(1, dtype=jnp.int32))
    tile = lambda t: jnp.tile(t, (1, RET_HEADS))
    vec = pl.BlockSpec((1, RET_HK), lambda i: (0, 0))
    state_spec = pl.BlockSpec((None, 1, RET_HK, RET_DV), lambda i: (li, i, 0, 0))
    in_specs = [
        pl.BlockSpec(memory_space=pltpu.SMEM),
        pl.BlockSpec((1, 1, n_in), lambda i: (i, 0, 0)),
        vec, vec, vec, vec,
        state_spec,
    ]
    args = [gamma, qkvg.reshape(b, 1, n_in), tile(cos_r), tile(sin_even), tile(sin_odd), gq,
            state_all.reshape(n_layers, b, RET_HK, RET_DV)]
    aliases = {}
    if new_states is not None:
        aliases = {len(args): 1}
        in_specs.append(pl.BlockSpec(memory_space=pl.ANY))
        args.append(new_states)
    x, s_new = pl.pallas_call(
        _ret_sample_kernel,
        out_shape=(jax.ShapeDtypeStruct((b, 1, RET_HV), BF16),
                   jax.ShapeDtypeStruct((n_layers, b, RET_HK, RET_DV), F32)),
        grid=(b,),
        in_specs=in_specs,
        out_specs=(pl.BlockSpec((1, 1, RET_HV), lambda i: (i, 0, 0)), state_spec),
        input_output_aliases=aliases,
        compiler_params=_params("parallel"),
        name="retention_sample",
    )(*args)
    return x.reshape(b, RET_HV), s_new


def _proj_res_kernel(xp_ref, xs_ref, w_ref, hp_ref, hs_ref, g_ref, hop_ref, hnp_ref, hos_ref, hns_ref, w_s,
                     *, k_rows, n_chunks, n_blocks):
    step = pl.program_id(0)
    chunk = w_ref.shape[0]

    @pl.when(step < n_chunks)
    def _():
        if w_s.shape[0] > k_rows:
            @pl.when(step == 0)
            def _():
                w_s[k_rows:, :] = jnp.zeros((w_s.shape[0] - k_rows, w_s.shape[1]), BF16)

        start = pl.multiple_of(step * chunk, 2 * SUBLANES)
        w_s[pl.ds(start, chunk), :] = w_ref[...].astype(BF16)

    def run(x_ref, h_ref, ho_ref, hn_ref):
        h_new = h_ref[...] + jnp.dot(x_ref[...], w_s[...], preferred_element_type=F32)
        ho_ref[...] = h_new
        hn_ref[...] = _rms(h_new, g_ref[...]).astype(hn_ref.dtype)

    @pl.when(jnp.logical_and(step >= n_chunks, step < n_chunks + n_blocks))
    def _():
        run(xp_ref, hp_ref, hop_ref, hnp_ref)

    @pl.when(step == n_chunks + n_blocks)
    def _():
        run(xs_ref, hs_ref, hos_ref, hns_ref)


def proj_residual(xp, xs, w_all, layer, hp, hs, g_next, chunk, name):
    tp, k = xp.shape
    ts = xs.shape[0]
    k_rows, d = w_all.shape[1], w_all.shape[2]
    n_chunks = k_rows // chunk
    assert chunk * n_chunks == k_rows and chunk % (2 * SUBLANES) == 0
    tm = PROJ_TM
    nb = tp // tm
    blk = lambda s: (jnp.clip(s - n_chunks, 0, nb - 1), 0)
    fixed = lambda s: (0, 0)
    f32 = lambda rows: jax.ShapeDtypeStruct((rows, d), F32)
    bf16 = lambda rows: jax.ShapeDtypeStruct((rows, d), BF16)
    return pl.pallas_call(
        functools.partial(_proj_res_kernel, k_rows=k_rows, n_chunks=n_chunks, n_blocks=nb),
        out_shape=(f32(tp), bf16(tp), f32(ts), bf16(ts)),
        grid=(n_chunks + nb + 1,),
        in_specs=[
            pl.BlockSpec((tm, k), blk),
            pl.BlockSpec((ts, k), fixed),
            pl.BlockSpec((None, chunk, d), lambda s: (layer, jnp.minimum(s, n_chunks - 1), 0)),
            pl.BlockSpec((tm, d), blk),
            pl.BlockSpec((ts, d), fixed),
            pl.BlockSpec((1, d), fixed),
        ],
        out_specs=(pl.BlockSpec((tm, d), blk), pl.BlockSpec((tm, d), blk),
                   pl.BlockSpec((ts, d), fixed), pl.BlockSpec((ts, d), fixed)),
        scratch_shapes=[pltpu.VMEM((k, d), BF16)],
        compiler_params=_params("arbitrary"),
        name=name,
    )(xp, xs, w_all, hp, hs, g_next.reshape(1, d))


def _conv_gate(a, a1, a2, b, cw_ref, cb_ref):
    conv = cb_ref[...] + cw_ref[0:1, :] * a2 + cw_ref[1:2, :] * a1 + cw_ref[2:3, :] * a
    return jax.nn.gelu(conv) * b


FFN_W_BLOCKS = FFN_TN // LANES
FFN_COL_BLOCKS = D_FF // LANES


def _ffn_weight_specs(layer, d, col_tile_of):
    last = 2 * FFN_COL_BLOCKS - 1

    def spec(first_block, r):
        def index(*grid_idx):
            return (layer, 0, jnp.minimum(first_block + FFN_W_BLOCKS * col_tile_of(*grid_idx) + r, last))
        return pl.BlockSpec((None, d, LANES), index)

    return ([spec(0, r) for r in range(FFN_W_BLOCKS)]
            + [spec(FFN_COL_BLOCKS, r) for r in range(FFN_W_BLOCKS)])


def _cast_weight_blocks(block_refs, dst):
    for r, ref in enumerate(block_refs):
        dst[:, r * LANES:(r + 1) * LANES] = ref[...].astype(BF16)


def _valid_cols(col_tile, shape):
    return col_tile * FFN_TN + lax.broadcasted_iota(jnp.int32, shape, 1) < D_FF


def _ffn_in_kernel(xp_ref, xs_ref, *refs, n_seq):
    wa_refs, wb_refs = refs[:FFN_W_BLOCKS], refs[FFN_W_BLOCKS:2 * FFN_W_BLOCKS]
    cw_ref, cb_ref, b0_ref, b1_ref, yp_ref, st_ref, ys_ref, as_ref, wa_s, wb_s = refs[2 * FFN_W_BLOCKS:]
    col_tile = pl.program_id(0)
    b = pl.program_id(1)

    @pl.when(b == 0)
    def _():
        _cast_weight_blocks(wa_refs, wa_s)
        _cast_weight_blocks(wb_refs, wb_s)

    @pl.when(b < n_seq)
    def _():
        seq = xp_ref.shape[0]
        rc = FFN_ROW_CHUNK
        tn = FFN_TN
        rows = lax.broadcasted_iota(jnp.int32, (rc, tn), 0)
        valid = _valid_cols(col_tile, (rc, tn))
        prev1 = jnp.zeros((1, tn), F32)
        prev2 = jnp.zeros((1, tn), F32)
        for r in range(seq // rc):
            x = xp_ref[r * rc:(r + 1) * rc, :]
            a = jnp.dot(x, wa_s[...], preferred_element_type=F32)
            g = jnp.dot(x, wb_s[...], preferred_element_type=F32)
            a1 = jnp.where(rows >= 1, pltpu.roll(a, 1, 0), prev1)
            a2 = jnp.where(rows >= 2, pltpu.roll(a, 2, 0), jnp.where(rows == 1, prev1, prev2))
            y = jnp.where(valid, _conv_gate(a, a1, a2, g, cw_ref, cb_ref), 0.0)
            yp_ref[r * rc:(r + 1) * rc, :] = y.astype(yp_ref.dtype)
            prev1 = a[rc - 1:rc, :]
            prev2 = a[rc - 2:rc - 1, :]
            if r == seq // rc - 1:
                st_ref[0] = a[rc - SUBLANES:, :]

    @pl.when(b == n_seq)
    def _():
        x = xs_ref[...]
        a = jnp.dot(x, wa_s[...], preferred_element_type=F32)
        g = jnp.dot(x, wb_s[...], preferred_element_type=F32)
        as_ref[...] = a
        y = jnp.where(_valid_cols(col_tile, a.shape),
                      _conv_gate(a, b1_ref[...], b0_ref[...], g, cw_ref, cb_ref), 0.0)
        ys_ref[...] = y.astype(ys_ref.dtype)


def ffn_in(hnp, hns, w_in_all, layer, cw, cb, buf, batch, seq):
    tn = FFN_TN
    d = hnp.shape[1]
    ts = hns.shape[0]
    pad = ((0, 0), (0, D_FF_PAD - D_FF))
    b0 = jnp.pad(buf[:, 0, :], pad)
    b1 = jnp.pad(buf[:, 1, :], pad)
    sq = lambda b: jnp.minimum(b, batch - 1)
    col = lambda rows_: pl.BlockSpec((rows_, tn), lambda j, b: (0, j))
    yp, st, ys, a_s = pl.pallas_call(
        functools.partial(_ffn_in_kernel, n_seq=batch),
        out_shape=(jax.ShapeDtypeStruct((batch * seq, D_FF_PAD), BF16),
                   jax.ShapeDtypeStruct((batch, SUBLANES, D_FF_PAD), F32),
                   jax.ShapeDtypeStruct((ts, D_FF_PAD), BF16),
                   jax.ShapeDtypeStruct((ts, D_FF_PAD), F32)),
        grid=(D_FF_PAD // tn, batch + 1),
        in_specs=[pl.BlockSpec((seq, d), lambda j, b: (sq(b), 0)), pl.BlockSpec((ts, d), lambda j, b: (0, 0))]
        + _ffn_weight_specs(layer, d, lambda j, b: j)
        + [col(CONV_W), col(1), col(ts), col(ts)],
        out_specs=(pl.BlockSpec((seq, tn), lambda j, b: (sq(b), j)),
                   pl.BlockSpec((1, SUBLANES, tn), lambda j, b: (sq(b), 0, j)),
                   col(ts), col(ts)),
        scratch_shapes=[pltpu.VMEM((d, tn), BF16), pltpu.VMEM((d, tn), BF16)],
        compiler_params=_params("arbitrary", "arbitrary"),
        name="ffn_in",
    )(hnp, hns, *([w_in_all] * (2 * FFN_W_BLOCKS)), cw, cb, b0, b1)
    conv_p = st[:, SUBLANES - (CONV_W - 1):, :D_FF]
    conv_s = jnp.stack([buf[:, 1, :], a_s[:, :D_FF]], axis=1)
    return yp, ys, conv_p, conv_s


def _ple_kernel(hnp_ref, hns_ref, wg_ref, pp_ref, ps_ref, wp_ref, hp_ref, hs_ref, g_ref, *refs, emit_h, n_blocks):
    out_refs, (wg_s, wp_s) = refs[:-2], refs[-2:]
    n_out = len(out_refs) // 2
    i = pl.program_id(0)

    @pl.when(i == 0)
    def _():
        wg_s[...] = wg_ref[...].astype(BF16)
        wp_s[...] = wp_ref[...].astype(BF16)

    def run(hn_ref, p_ref, h_ref, outs):
        gate = jax.nn.sigmoid(jnp.dot(hn_ref[...], wg_s[...], preferred_element_type=F32))
        proj = jnp.dot(p_ref[...].astype(BF16), wp_s[...], preferred_element_type=F32)
        h_new = h_ref[...] + gate * proj
        if emit_h:
            outs[0][...] = h_new
        outs[-1][...] = _rms(h_new, g_ref[...]).astype(outs[-1].dtype)

    @pl.when(i < n_blocks)
    def _():
        run(hnp_ref, pp_ref, hp_ref, out_refs[:n_out])

    @pl.when(i == n_blocks)
    def _():
        run(hns_ref, ps_ref, hs_ref, out_refs[n_out:])


def ple_residual(hnp, hns, wg_all, pp_all, ps_all, layer, wp_all, hp, hs, g_next, emit_h):
    tp, d = hp.shape
    ts = hs.shape[0]
    pd = pp_all.shape[-1]
    pp_all = pp_all.reshape(pp_all.shape[0], tp, pd)
    ps_all = ps_all.reshape(ps_all.shape[0], ts, pd)
    tm = PLE_TM
    nb = tp // tm
    blk = lambda i: jnp.minimum(i, nb - 1)
    prow = lambda width: pl.BlockSpec((tm, width), lambda i: (blk(i), 0))
    srow = lambda width: pl.BlockSpec((ts, width), lambda i: (0, 0))
    const = lambda r, c: pl.BlockSpec((None, r, c), lambda i: (layer, 0, 0), pipeline_mode=pl.Buffered(1))
    if emit_h:
        out_shape = tuple(jax.ShapeDtypeStruct((rows, d), dt) for rows in (tp, ts) for dt in (F32, BF16))
        out_specs = (prow(d), prow(d), srow(d), srow(d))
    else:
        out_shape = (jax.ShapeDtypeStruct((tp, d), F32), jax.ShapeDtypeStruct((ts, d), F32))
        out_specs = (prow(d), srow(d))
    return pl.pallas_call(
        functools.partial(_ple_kernel, emit_h=emit_h, n_blocks=nb),
        out_shape=out_shape,
        grid=(nb + 1,),
        in_specs=[prow(d), srow(d), const(d, d),
                  pl.BlockSpec((None, tm, pd), lambda i: (layer, blk(i), 0)),
                  pl.BlockSpec((None, ts, pd), lambda i: (layer, 0, 0)),
                  const(pd, d), prow(d), srow(d), pl.BlockSpec((1, d), lambda i: (0, 0))],
        out_specs=out_specs,
        scratch_shapes=[pltpu.VMEM((d, d), BF16), pltpu.VMEM((pd, d), BF16)],
        compiler_params=_params("arbitrary"),
        name="ple_residual",
    )(hnp, hns, wg_all, pp_all, ps_all, wp_all, hp, hs, g_next.reshape(1, d))


def _trunks(x_prompt, x_sample, state_ret, state_conv, p_prompt, p_sample, w):
    bp, lp = x_prompt.shape[0], x_prompt.shape[1]
    bs, ls = x_sample.shape[0], x_sample.shape[1]
    hp = x_prompt.reshape(bp * lp, D_MODEL)
    hs = x_sample.reshape(bs * ls, D_MODEL)
    hnp = rmsnorm_rows(hp, w["norm_mix_g"][0], BF16)
    hns = rmsnorm_rows(hs, w["norm_mix_g"][0], BF16)
    ret_p, conv_p, conv_s, v_s = [], [], [], []
    ret_s = None
    for i in range(DEPTH):
        li = i // N_MIXERS
        if i % N_MIXERS == 0:
            zp, zs = matmul_act(hnp, hns, w["gmlp_w_in"], li, "gelu", "gmlp_in")
            gate_w = (w["gmlp_ln_g"][li], w["gmlp_ln_b"][li], w["gmlp_w_s"][li], w["gmlp_b_s"][li])
            xmp = gmlp_gate_prompt(zp, *gate_w)
            xms, vn = gmlp_gate_sample(zs, *gate_w)
            v_s.append(vn.reshape(bs, ls, GMLP_WIDTH))
            w_out = w["gmlp_w_out"]
        else:
            qp, qs = matmul_act(hnp, hns, w["ret_w_in"], li, None, "ret_in")
            xmp, s_new = retention_prompt(qp, bp, lp)
            ret_p.append(s_new)
            xms, ret_s = retention_sample(qs, state_ret, li, ret_s)
            w_out = w["ret_w_out"]
        hp, hnp, hs, hns = proj_residual(xmp, xms, w_out, li, hp, hs, w["norm_ffn_g"][i],
                                         MIXER_OUT_W_CHUNK, "mixer_out")
        yp, ys, cp, cs = ffn_in(hnp, hns, w["ffn_w_in"], i, w["ffn_cw"][i], w["ffn_cb"][i], state_conv[i], bp, lp)
        conv_p.append(cp)
        conv_s.append(cs)
        hp, hnp, hs, hns = proj_residual(yp, ys, w["ffn_w_down"], i, hp, hs, w["norm_ple_g"][i],
                                         FFN_DOWN_W_CHUNK, "ffn_down")
        last = i == DEPTH - 1
        g_next = w["final_norm_g"] if last else w["norm_mix_g"][i + 1]
        outs = ple_residual(hnp, hns, w["ple_w_gate"], p_prompt, p_sample, i, w["ple_w_proj"], hp, hs, g_next,
                            not last)
        if last:
            y_p, y_s = outs
        else:
            hp, hnp, hs, hns = outs
    return (y_p.reshape(bp, lp, D_MODEL), y_s.reshape(bs, ls, D_MODEL),
            jnp.stack(ret_p), ret_s.reshape(state_ret.shape),
            jnp.stack(conv_p), jnp.stack(conv_s), jnp.stack(v_s))


def kernel(x_prompt, x_sample, state_ret, state_conv, p_prompt, p_sample, norm_mix_g, norm_ffn_g, norm_ple_g, final_norm_g, gmlp_w_in, gmlp_ln_g, gmlp_ln_b, gmlp_w_s, gmlp_b_s, gmlp_w_out, ret_w_in, ret_w_out, ffn_w_in, ffn_conv_w, ffn_conv_b, ffn_w_down, ple_w_proj, ple_w_gate):
    ff_pad = D_FF_PAD - D_FF
    w = {
        "norm_mix_g": norm_mix_g, "norm_ffn_g": norm_ffn_g, "norm_ple_g": norm_ple_g, "final_norm_g": final_norm_g,
        "gmlp_w_in": gmlp_w_in, "gmlp_ln_g": gmlp_ln_g, "gmlp_ln_b": gmlp_ln_b,
        "gmlp_w_s": gmlp_w_s, "gmlp_b_s": gmlp_b_s, "gmlp_w_out": gmlp_w_out,
        "ret_w_in": ret_w_in, "ret_w_out": ret_w_out,
        "ffn_w_in": ffn_w_in,
        "ffn_cw": jnp.pad(ffn_conv_w, ((0, 0), (0, 0), (0, ff_pad))),
        "ffn_cb": jnp.pad(ffn_conv_b, ((0, 0), (0, ff_pad))).reshape(DEPTH, 1, D_FF_PAD),
        "ffn_w_down": ffn_w_down,
        "ple_w_proj": ple_w_proj, "ple_w_gate": ple_w_gate,
    }
    return _trunks(x_prompt, x_sample, state_ret, state_conv, p_prompt, p_sample, w)
```

```python
import functools
import math

import jax
import jax.numpy as jnp
from jax import lax
from jax.experimental import pallas as pl
from jax.experimental.pallas import tpu as pltpu

F32 = jnp.float32
BF16 = jnp.bfloat16

D_MODEL = 2048
DEPTH = 4
PAST_LEN = 16384
N_MIXERS = 2
PLE_DIM = 256
GMLP_CHUNK = 128
GMLP_WIDTH = 2 * D_MODEL
GMLP_GROUPS = 8
GMLP_GROUP_DIM = GMLP_WIDTH // GMLP_GROUPS
RET_HEADS = 8
RET_DK = D_MODEL // RET_HEADS
RET_DV = 2 * RET_DK
RET_HK = RET_HEADS * RET_DK
RET_HV = RET_HEADS * RET_DV
RET_CHUNK = 128
ROT_BASE = 10000.0
D_FF = ((8 * D_MODEL // 3 + 127) // 128) * 128
CONV_W = 3
EPS = 1e-6

V7X_VMEM_BYTES = 64 * 1024 * 1024
VMEM_LIMIT_BYTES = V7X_VMEM_BYTES - 8 * 1024 * 1024
SUBLANES = 8
LANES = 128
MXU_DEPTH = 256
D_FF_PAD = -(-D_FF // (2 * MXU_DEPTH)) * (2 * MXU_DEPTH)
FFN_TN = 512
FFN_ROW_CHUNK = 256
MM_TM = 1024
MM_TN = 1024
PROJ_TM = 256
PLE_TM = 256
MIXER_OUT_W_CHUNK = 512
FFN_DOWN_W_CHUNK = 128


def _params(*sem):
    return pltpu.CompilerParams(dimension_semantics=sem, vmem_limit_bytes=VMEM_LIMIT_BYTES)


def _rms(x, g):
    r = lax.rsqrt(jnp.mean(x * x, axis=-1, keepdims=True) + EPS)
    return x * r * g


def _group_norm(o):
    mu = jnp.mean(o, axis=-1, keepdims=True)
    d = o - mu
    var = jnp.mean(d * d, axis=-1, keepdims=True)
    return d * lax.rsqrt(var + EPS)


def _rotate(x, cos, sin_even, sin_odd):
    n = x.shape[-1]
    return x * cos + pltpu.roll(x, n - 1, 1) * sin_even + pltpu.roll(x, 1, 1) * sin_odd


def _rms_kernel(x_ref, g_ref, o_ref):
    o_ref[...] = _rms(x_ref[...], g_ref[...]).astype(o_ref.dtype)


def rmsnorm_rows(x, g, out_dtype):
    t, d = x.shape
    tm = min(256, t)
    return pl.pallas_call(
        _rms_kernel,
        out_shape=jax.ShapeDtypeStruct((t, d), out_dtype),
        grid=(t // tm,),
        in_specs=[pl.BlockSpec((tm, d), lambda i: (i, 0)), pl.BlockSpec((1, d), lambda i: (0, 0))],
        out_specs=pl.BlockSpec((tm, d), lambda i: (i, 0)),
        compiler_params=_params("parallel"),
        name="rmsnorm_rows",
    )(x, g.reshape(1, d))


def _mm_kernel(xp_ref, xs_ref, w_ref, op_ref, os_ref, w_s, *, act):
    i = pl.program_id(1)

    def run(x_ref, o_ref):
        acc = jnp.dot(x_ref[...], w_s[...], preferred_element_type=F32)
        if act == "gelu":
            acc = jax.nn.gelu(acc)
        o_ref[...] = acc.astype(o_ref.dtype)

    @pl.when(i == 0)
    def _():
        w_s[...] = w_ref[...].astype(BF16)
        run(xs_ref, os_ref)

    @pl.when(i > 0)
    def _():
        run(xp_ref, op_ref)


def matmul_act(xp, xs, w_all, layer, act, prompt_dtype, name):
    tp, k = xp.shape
    ts = xs.shape[0]
    n = w_all.shape[2]
    tm = MM_TM
    tn = MM_TN
    nb = tp // tm
    blk = lambda i: jnp.maximum(i - 1, 0)
    return pl.pallas_call(
        functools.partial(_mm_kernel, act=act),
        out_shape=(jax.ShapeDtypeStruct((tp, n), prompt_dtype), jax.ShapeDtypeStruct((ts, n), F32)),
        grid=(n // tn, nb + 1),
        in_specs=[pl.BlockSpec((tm, k), lambda j, i: (blk(i), 0)),
                  pl.BlockSpec((ts, k), lambda j, i: (0, 0)),
                  pl.BlockSpec((None, k, tn), lambda j, i: (layer, 0, j))],
        out_specs=(pl.BlockSpec((tm, tn), lambda j, i: (blk(i), j)),
                   pl.BlockSpec((ts, tn), lambda j, i: (0, j))),
        scratch_shapes=[pltpu.VMEM((k, tn), BF16)],
        compiler_params=_params("arbitrary", "arbitrary"),
        name=name,
    )(xp, xs, w_all)


def _layernorm(v, g, b):
    mu = jnp.mean(v, axis=-1, keepdims=True)
    d = v - mu
    var = jnp.mean(d * d, axis=-1, keepdims=True)
    return d * lax.rsqrt(var + EPS) * g + b


def _gmlp_gate_kernel(u_ref, v_ref, lng_ref, lnb_ref, ws_ref, bs_ref, x_ref):
    vn = _layernorm(v_ref[...].astype(F32), lng_ref[...], lnb_ref[...]).astype(BF16)
    c = GMLP_CHUNK
    causal = lax.broadcasted_iota(jnp.int32, (c, c), 0) >= lax.broadcasted_iota(jnp.int32, (c, c), 1)
    for g in range(GMLP_GROUPS):
        cols = slice(g * GMLP_GROUP_DIM, (g + 1) * GMLP_GROUP_DIM)
        w = jnp.where(causal, ws_ref[g], 0.0).astype(BF16)
        s = jnp.dot(w, vn[:, cols], preferred_element_type=F32) + bs_ref[g]
        x_ref[:, cols] = (u_ref[:, cols].astype(F32) * s).astype(x_ref.dtype)


def gmlp_gate_prompt(z, ln_g, ln_b, w_s, b_s):
    t = z.shape[0]
    w = GMLP_WIDTH
    c = GMLP_CHUNK
    bs_b = jnp.broadcast_to(b_s[:, :, None], (GMLP_GROUPS, c, GMLP_GROUP_DIM))
    return pl.pallas_call(
        _gmlp_gate_kernel,
        out_shape=jax.ShapeDtypeStruct((t, w), BF16),
        grid=(t // c,),
        in_specs=[
            pl.BlockSpec((c, w), lambda i: (i, 0)),
            pl.BlockSpec((c, w), lambda i: (i, 1)),
            pl.BlockSpec((1, w), lambda i: (0, 0)),
            pl.BlockSpec((1, w), lambda i: (0, 0)),
            pl.BlockSpec((GMLP_GROUPS, c, c), lambda i: (0, 0, 0)),
            pl.BlockSpec((GMLP_GROUPS, c, GMLP_GROUP_DIM), lambda i: (0, 0, 0)),
        ],
        out_specs=pl.BlockSpec((c, w), lambda i: (i, 0)),
        compiler_params=_params("parallel"),
        name="gmlp_gate_prompt",
    )(z, z, ln_g.reshape(1, w), ln_b.reshape(1, w), w_s, bs_b)


def _gmlp_gate_sample_kernel(u_ref, v_ref, lng_ref, lnb_ref, scale_ref, shift_ref, x_ref, vn_ref):
    vn = _layernorm(v_ref[...], lng_ref[...], lnb_ref[...])
    vn_ref[...] = vn
    x_ref[...] = (u_ref[...] * (vn * scale_ref[...] + shift_ref[...])).astype(x_ref.dtype)


def gmlp_gate_sample(z, ln_g, ln_b, w_s, b_s):
    t = z.shape[0]
    w = GMLP_WIDTH
    scale = jnp.repeat(w_s[:, 0, 0], GMLP_GROUP_DIM).reshape(1, w)
    shift = jnp.repeat(b_s[:, 0], GMLP_GROUP_DIM).reshape(1, w)
    row = pl.BlockSpec((1, w), lambda i: (0, 0))
    return pl.pallas_call(
        _gmlp_gate_sample_kernel,
        out_shape=(jax.ShapeDtypeStruct((t, w), BF16), jax.ShapeDtypeStruct((t, w), F32)),
        grid=(1,),
        in_specs=[pl.BlockSpec((t, w), lambda i: (0, 0)), pl.BlockSpec((t, w), lambda i: (0, 1)), row, row, row, row],
        out_specs=(pl.BlockSpec((t, w), lambda i: (0, 0)), pl.BlockSpec((t, w), lambda i: (0, 0))),
        compiler_params=_params("arbitrary"),
        name="gmlp_gate_sample",
    )(z, z, ln_g.reshape(1, w), ln_b.reshape(1, w), scale, shift)


def _rotary_tables(pos):
    angle = 1.0 / (ROT_BASE ** jnp.linspace(0.0, 1.0, RET_DK // 2, dtype=F32))
    ph = pos.astype(F32)[:, None] * angle[None, :]
    cos = jnp.cos(ph)
    sin = jnp.sin(ph)
    zero = jnp.zeros_like(sin)
    n = pos.shape[0]
    cos_r = jnp.stack([cos, cos], axis=-1).reshape(n, RET_DK)
    sin_even = jnp.stack([-sin, zero], axis=-1).reshape(n, RET_DK)
    sin_odd = jnp.stack([zero, sin], axis=-1).reshape(n, RET_DK)
    return cos_r, sin_even, sin_odd


def _log_gamma():
    return jnp.log(1.0 - 2.0 ** (-5.0 - jnp.arange(RET_HEADS, dtype=F32)))


def _ret_prompt_heads(heads, cdec_ref, q_ref, k_ref, v_ref, g_ref, cos_ref, se_ref, so_ref,
                      din_ref, qd_ref, kd_ref, x_ref, s_ref):
    cos = cos_ref[...]
    se = se_ref[...]
    so = so_ref[...]
    for h in heads:
        kc = slice(h * RET_DK, (h + 1) * RET_DK)
        vc = slice(h * RET_DV, (h + 1) * RET_DV)
        q = _rotate(q_ref[:, kc], cos, se, so)
        k = _rotate(k_ref[:, kc], cos, se, so) * (RET_DK ** -0.5)
        v = v_ref[:, vc].astype(BF16)
        scores = lax.dot_general(q.astype(BF16), k.astype(BF16), (((1,), (1,)), ((), ())),
                                 preferred_element_type=F32) * din_ref[h]
        s_old = s_ref[0, h]
        lhs = jnp.concatenate([scores.astype(BF16), (q * qd_ref[h]).astype(BF16)], axis=1)
        rhs = jnp.concatenate([v, s_old.astype(BF16)], axis=0)
        o = jnp.dot(lhs, rhs, preferred_element_type=F32)
        kt = jnp.transpose(k * kd_ref[h]).astype(BF16)
        s_ref[0, h] = cdec_ref[h] * s_old + jnp.dot(kt, v, preferred_element_type=F32)
        on = _group_norm(o)
        x_ref[:, vc] = (jax.nn.silu(g_ref[:, vc]) * on).astype(x_ref.dtype)


def _ret_sample_token(cdec_ref, row_ref, cos_ref, se_ref, so_ref, gq_ref, s_ref, x_ref, so_out_ref):
    row = row_ref[0]
    q = _rotate(row[:, 0:RET_HK], cos_ref[...], se_ref[...], so_ref[...])
    k = _rotate(row[:, RET_HK:2 * RET_HK], cos_ref[...], se_ref[...], so_ref[...]) * (RET_DK ** -0.5)
    v = row[:, 2 * RET_HK:2 * RET_HK + RET_HV]
    g = row[:, 2 * RET_HK + RET_HV:]
    qk = q * k

    pad = LANES
    head_of_row = lax.broadcasted_iota(jnp.int32, (pad, RET_HK), 0)
    head_of_col = lax.shift_right_logical(lax.broadcasted_iota(jnp.int32, (pad, RET_HK), 1),
                                          int(math.log2(RET_DK)))
    on_diag = head_of_row == head_of_col
    q_bd = jnp.where(on_diag[:SUBLANES], q * gq_ref[...], 0.0).astype(BF16)
    k_bd = jnp.where(on_diag, k, 0.0)
    s_old = s_ref[0]
    cross = jnp.dot(q_bd, s_old.astype(BF16), preferred_element_type=F32)
    v_row_id = lax.broadcasted_iota(jnp.int32, (pad, RET_DV), 0)
    v_rows = jnp.zeros((pad, RET_DV), F32)
    for h in range(RET_HEADS):
        v_rows = jnp.where(v_row_id == h, v[:, h * RET_DV:(h + 1) * RET_DV], v_rows)
    k_cols = jnp.transpose(k_bd).astype(BF16)
    v_rows = v_rows.astype(BF16)
    for h in range(RET_HEADS):
        kc = slice(h * RET_DK, (h + 1) * RET_DK)
        vc = slice(h * RET_DV, (h + 1) * RET_DV)
        score = jnp.sum(qk[:, kc], axis=-1, keepdims=True)
        on = _group_norm(score * v[:, vc] + cross[h:h + 1, :])
        x_ref[0, :, vc] = (jax.nn.silu(g[:, vc]) * on).astype(x_ref.dtype)
        outer = jnp.dot(k_cols[kc, :], v_rows, preferred_element_type=F32)
        so_out_ref[0, kc, :] = cdec_ref[h] * s_old[kc, :] + outer


RET_HEAD_SPLITS = 2
N_PROMPT_REFS = 13
N_SAMPLE_IN_REFS = 7


def _retention_kernel(*refs):
    p_in = refs[:N_PROMPT_REFS - 2]
    s_in = refs[N_PROMPT_REFS - 2:N_PROMPT_REFS - 2 + N_SAMPLE_IN_REFS]
    xp_ref, sp_ref, xs_ref, ss_ref = refs[-4:]
    group = pl.program_id(2)

    @pl.when(jnp.logical_and(pl.program_id(1) == 0, group == 0))
    def _():
        sp_ref[...] = jnp.zeros_like(sp_ref)

    per_group = RET_HEADS // RET_HEAD_SPLITS
    for gi in range(RET_HEAD_SPLITS):
        @pl.when(group == gi)
        def _(gi=gi):
            _ret_prompt_heads(range(gi * per_group, (gi + 1) * per_group), *p_in, xp_ref, sp_ref)
            _ret_sample_token(*s_in, xs_ref, ss_ref)


def retention(qkvg_p, qkvg_s, batch, seq, state_all, li, new_states):
    c = RET_CHUNK
    n = seq // c
    bs, n_in = qkvg_s.shape
    n_layers = state_all.shape[0]
    assert batch * n * RET_HEAD_SPLITS == bs, "one sample token per grid step"
    lg = _log_gamma()
    idx = jnp.arange(c, dtype=F32)
    diff = idx[:, None] - idx[None, :]
    decay_in = jnp.where(diff[None] >= 0, jnp.exp(lg[:, None, None] * jnp.maximum(diff, 0.0)[None]), 0.0)
    q_dec = jnp.exp(lg[:, None] * (idx + 1.0)[None, :])
    k_dec = jnp.exp(lg[:, None] * (c - 1.0 - idx)[None, :])
    chunk_dec = jnp.exp(lg * c)
    qd = jnp.broadcast_to(q_dec[:, :, None], (RET_HEADS, c, RET_DK))
    kd = jnp.broadcast_to(k_dec[:, :, None], (RET_HEADS, c, RET_DK))
    cos_p, se_p, so_p = _rotary_tables(jnp.arange(seq, dtype=jnp.int32))
    gamma = jnp.exp(lg * 1.0)
    gq = jnp.repeat(gamma, RET_DK).reshape(1, RET_HK)
    cos_s, se_s, so_s = (jnp.tile(t, (1, RET_HEADS))
                         for t in _rotary_tables(PAST_LEN + jnp.arange(1, dtype=jnp.int32)))

    rows = lambda b, j, g: b * n + j
    tok = lambda b, j, g: (b * n + j) * RET_HEAD_SPLITS + g
    smem = pl.BlockSpec(memory_space=pltpu.SMEM)
    tab = pl.BlockSpec((c, RET_DK), lambda b, j, g: (j, 0))
    once = lambda shape: pl.BlockSpec(shape, lambda b, j, g: (0,) * len(shape), pipeline_mode=pl.Buffered(1))
    vec = once((1, RET_HK))
    state_spec = pl.BlockSpec((None, 1, RET_HK, RET_DV), lambda b, j, g: (li, tok(b, j, g), 0, 0))
    in_specs = [
        smem,
        pl.BlockSpec((c, RET_HK), lambda b, j, g: (rows(b, j, g), 0)),
        pl.BlockSpec((c, RET_HK), lambda b, j, g: (rows(b, j, g), 1)),
        pl.BlockSpec((c, RET_HV), lambda b, j, g: (rows(b, j, g), 1)),
        pl.BlockSpec((c, RET_HV), lambda b, j, g: (rows(b, j, g), 2)),
        tab, tab, tab,
        once((RET_HEADS, c, c)), once((RET_HEADS, c, RET_DK)), once((RET_HEADS, c, RET_DK)),
        smem,
        pl.BlockSpec((1, 1, n_in), lambda b, j, g: (tok(b, j, g), 0, 0)),
        vec, vec, vec, vec,
        state_spec,
    ]
    args = [chunk_dec, qkvg_p, qkvg_p, qkvg_p, qkvg_p, cos_p, se_p, so_p, decay_in, qd, kd,
            gamma, qkvg_s.reshape(bs, 1, n_in), cos_s, se_s, so_s, gq,
            state_all.reshape(n_layers, bs, RET_HK, RET_DV)]
    assert len(args) == N_PROMPT_REFS - 2 + N_SAMPLE_IN_REFS
    aliases = {}
    if new_states is not None:
        aliases = {len(args): 3}
        in_specs.append(pl.BlockSpec(memory_space=pl.ANY))
        args.append(new_states)
    xp, sp, xs, ss = pl.pallas_call(
        _retention_kernel,
        out_shape=(jax.ShapeDtypeStruct((batch * seq, RET_HV), BF16),
                   jax.ShapeDtypeStruct((batch, RET_HEADS, RET_DK, RET_DV), F32),
                   jax.ShapeDtypeStruct((bs, 1, RET_HV), BF16),
                   jax.ShapeDtypeStruct((n_layers, bs, RET_HK, RET_DV), F32)),
        grid=(batch, n, RET_HEAD_SPLITS),
        in_specs=in_specs,
        out_specs=(pl.BlockSpec((c, RET_HV), lambda b, j, g: (rows(b, j, g), 0)),
                   pl.BlockSpec((1, RET_HEADS, RET_DK, RET_DV), lambda b, j, g: (b, 0, 0, 0)),
                   pl.BlockSpec((1, 1, RET_HV), lambda b, j, g: (tok(b, j, g), 0, 0)),
                   state_spec),
        input_output_aliases=aliases,
        compiler_params=_params("arbitrary", "arbitrary", "arbitrary"),
        name="retention",
    )(*args)
    return xp, sp, xs.reshape(bs, RET_HV), ss


def _proj_res_kernel(xp_ref, xs_ref, w_ref, hp_ref, hs_ref, g_ref, hop_ref, hnp_ref, hos_ref, hns_ref, w_s,
                     *, k_rows, n_chunks, n_blocks):
    step = pl.program_id(0)
    chunk = w_ref.shape[0]

    @pl.when(step < n_chunks)
    def _():
        if w_s.shape[0] > k_rows:
            @pl.when(step == 0)
            def _():
                w_s[k_rows:, :] = jnp.zeros((w_s.shape[0] - k_rows, w_s.shape[1]), BF16)

        start = pl.multiple_of(step * chunk, 2 * SUBLANES)
        w_s[pl.ds(start, chunk), :] = w_ref[...].astype(BF16)

    def run(x_ref, h_ref, ho_ref, hn_ref):
        h_new = h_ref[...] + jnp.dot(x_ref[...], w_s[...], preferred_element_type=F32)
        ho_ref[...] = h_new
        hn_ref[...] = _rms(h_new, g_ref[...]).astype(hn_ref.dtype)

    @pl.when(jnp.logical_and(step >= n_chunks, step < n_chunks + n_blocks))
    def _():
        run(xp_ref, hp_ref, hop_ref, hnp_ref)

    @pl.when(step == n_chunks + n_blocks)
    def _():
        run(xs_ref, hs_ref, hos_ref, hns_ref)


def proj_residual(xp, xs, w_all, layer, hp, hs, g_next, chunk, name):
    tp, k = xp.shape
    ts = xs.shape[0]
    k_rows, d = w_all.shape[1], w_all.shape[2]
    n_chunks = k_rows // chunk
    assert chunk * n_chunks == k_rows and chunk % (2 * SUBLANES) == 0
    tm = PROJ_TM
    nb = tp // tm
    blk = lambda s: (jnp.clip(s - n_chunks, 0, nb - 1), 0)
    fixed = lambda s: (0, 0)
    f32 = lambda rows: jax.ShapeDtypeStruct((rows, d), F32)
    bf16 = lambda rows: jax.ShapeDtypeStruct((rows, d), BF16)
    return pl.pallas_call(
        functools.partial(_proj_res_kernel, k_rows=k_rows, n_chunks=n_chunks, n_blocks=nb),
        out_shape=(f32(tp), bf16(tp), f32(ts), bf16(ts)),
        grid=(n_chunks + nb + 1,),
        in_specs=[
            pl.BlockSpec((tm, k), blk),
            pl.BlockSpec((ts, k), fixed),
            pl.BlockSpec((None, chunk, d), lambda s: (layer, jnp.minimum(s, n_chunks - 1), 0)),
            pl.BlockSpec((tm, d), blk),
            pl.BlockSpec((ts, d), fixed),
            pl.BlockSpec((1, d), fixed),
        ],
        out_specs=(pl.BlockSpec((tm, d), blk), pl.BlockSpec((tm, d), blk),
                   pl.BlockSpec((ts, d), fixed), pl.BlockSpec((ts, d), fixed)),
        scratch_shapes=[pltpu.VMEM((k, d), BF16)],
        compiler_params=_params("arbitrary"),
        name=name,
    )(xp, xs, w_all, hp, hs, g_next.reshape(1, d))


def _conv_gate(a, a1, a2, b, cw_ref, cb_ref):
    conv = cb_ref[...] + cw_ref[0:1, :] * a2 + cw_ref[1:2, :] * a1 + cw_ref[2:3, :] * a
    return jax.nn.gelu(conv) * b


FFN_W_BLOCKS = FFN_TN // LANES
FFN_COL_BLOCKS = D_FF // LANES


def _ffn_weight_specs(layer, d, col_tile_of):
    last = 2 * FFN_COL_BLOCKS - 1

    def spec(first_block, r):
        def index(*grid_idx):
            return (layer, 0, jnp.minimum(first_block + FFN_W_BLOCKS * col_tile_of(*grid_idx) + r, last))
        return pl.BlockSpec((None, d, LANES), index)

    return ([spec(0, r) for r in range(FFN_W_BLOCKS)]
            + [spec(FFN_COL_BLOCKS, r) for r in range(FFN_W_BLOCKS)])


def _cast_weight_blocks(block_refs, dst):
    for r, ref in enumerate(block_refs):
        dst[:, r * LANES:(r + 1) * LANES] = ref[...].astype(BF16)


def _valid_cols(col_tile, shape):
    return col_tile * FFN_TN + lax.broadcasted_iota(jnp.int32, shape, 1) < D_FF


def _ffn_in_kernel(xp_ref, xs_ref, *refs):
    wa_refs, wb_refs = refs[:FFN_W_BLOCKS], refs[FFN_W_BLOCKS:2 * FFN_W_BLOCKS]
    cw_ref, cb_ref, b0_ref, b1_ref, yp_ref, st_ref, ys_ref, as_ref, wa_s, wb_s = refs[2 * FFN_W_BLOCKS:]
    col_tile = pl.program_id(0)
    b = pl.program_id(1)

    @pl.when(b == 0)
    def _():
        _cast_weight_blocks(wa_refs, wa_s)
        _cast_weight_blocks(wb_refs, wb_s)
        x = xs_ref[...]
        a = jnp.dot(x, wa_s[...], preferred_element_type=F32)
        g = jnp.dot(x, wb_s[...], preferred_element_type=F32)
        as_ref[...] = a
        y = jnp.where(_valid_cols(col_tile, a.shape),
                      _conv_gate(a, b1_ref[...], b0_ref[...], g, cw_ref, cb_ref), 0.0)
        ys_ref[...] = y.astype(ys_ref.dtype)

    @pl.when(b > 0)
    def _():
        seq = xp_ref.shape[0]
        rc = FFN_ROW_CHUNK
        tn = FFN_TN
        rows = lax.broadcasted_iota(jnp.int32, (rc, tn), 0)
        valid = _valid_cols(col_tile, (rc, tn))
        prev1 = jnp.zeros((1, tn), F32)
        prev2 = jnp.zeros((1, tn), F32)
        for r in range(seq // rc):
            x = xp_ref[r * rc:(r + 1) * rc, :]
            a = jnp.dot(x, wa_s[...], preferred_element_type=F32)
            g = jnp.dot(x, wb_s[...], preferred_element_type=F32)
            a1 = jnp.where(rows >= 1, pltpu.roll(a, 1, 0), prev1)
            a2 = jnp.where(rows >= 2, pltpu.roll(a, 2, 0), jnp.where(rows == 1, prev1, prev2))
            y = jnp.where(valid, _conv_gate(a, a1, a2, g, cw_ref, cb_ref), 0.0)
            yp_ref[r * rc:(r + 1) * rc, :] = y.astype(yp_ref.dtype)
            prev1 = a[rc - 1:rc, :]
            prev2 = a[rc - 2:rc - 1, :]
            if r == seq // rc - 1:
                st_ref[0] = a[rc - SUBLANES:, :]


def ffn_in(hnp, hns, w_in_all, layer, cw, cb, buf, batch, seq):
    tn = FFN_TN
    d = hnp.shape[1]
    ts = hns.shape[0]
    pad = ((0, 0), (0, D_FF_PAD - D_FF))
    b0 = jnp.pad(buf[:, 0, :], pad)
    b1 = jnp.pad(buf[:, 1, :], pad)
    sq = lambda b: jnp.maximum(b - 1, 0)
    col = lambda rows_: pl.BlockSpec((rows_, tn), lambda j, b: (0, j))
    yp, st, ys, a_s = pl.pallas_call(
        _ffn_in_kernel,
        out_shape=(jax.ShapeDtypeStruct((batch * seq, D_FF_PAD), BF16),
                   jax.ShapeDtypeStruct((batch, SUBLANES, D_FF_PAD), F32),
                   jax.ShapeDtypeStruct((ts, D_FF_PAD), BF16),
                   jax.ShapeDtypeStruct((ts, D_FF_PAD), F32)),
        grid=(D_FF_PAD // tn, batch + 1),
        in_specs=[pl.BlockSpec((seq, d), lambda j, b: (sq(b), 0)), pl.BlockSpec((ts, d), lambda j, b: (0, 0))]
        + _ffn_weight_specs(layer, d, lambda j, b: j)
        + [col(CONV_W), col(1), col(ts), col(ts)],
        out_specs=(pl.BlockSpec((seq, tn), lambda j, b: (sq(b), j)),
                   pl.BlockSpec((1, SUBLANES, tn), lambda j, b: (sq(b), 0, j)),
                   col(ts), col(ts)),
        scratch_shapes=[pltpu.VMEM((d, tn), BF16), pltpu.VMEM((d, tn), BF16)],
        compiler_params=_params("arbitrary", "arbitrary"),
        name="ffn_in",
    )(hnp, hns, *([w_in_all] * (2 * FFN_W_BLOCKS)), cw, cb, b0, b1)
    conv_p = st[:, SUBLANES - (CONV_W - 1):, :D_FF]
    conv_s = jnp.stack([buf[:, 1, :], a_s[:, :D_FF]], axis=1)
    return yp, ys, conv_p, conv_s


def _ple_kernel(hnp_ref, hns_ref, wg_ref, pp_ref, ps_ref, wp_ref, hp_ref, hs_ref, g_ref, *refs, emit_h, n_blocks):
    out_refs, (wg_s, wp_s) = refs[:-2], refs[-2:]
    n_out = len(out_refs) // 2
    i = pl.program_id(0)

    @pl.when(i == 0)
    def _():
        wg_s[...] = wg_ref[...].astype(BF16)
        wp_s[...] = wp_ref[...].astype(BF16)

    def run(hn_ref, p_ref, h_ref, outs):
        gate = jax.nn.sigmoid(jnp.dot(hn_ref[...], wg_s[...], preferred_element_type=F32))
        proj = jnp.dot(p_ref[...].astype(BF16), wp_s[...], preferred_element_type=F32)
        h_new = h_ref[...] + gate * proj
        if emit_h:
            outs[0][...] = h_new
        outs[-1][...] = _rms(h_new, g_ref[...]).astype(outs[-1].dtype)

    @pl.when(i < n_blocks)
    def _():
        run(hnp_ref, pp_ref, hp_ref, out_refs[:n_out])

    @pl.when(i == n_blocks)
    def _():
        run(hns_ref, ps_ref, hs_ref, out_refs[n_out:])


def ple_residual(hnp, hns, wg_all, pp_all, ps_all, layer, wp_all, hp, hs, g_next, emit_h):
    tp, d = hp.shape
    ts = hs.shape[0]
    pd = pp_all.shape[-1]
    pp_all = pp_all.reshape(pp_all.shape[0], tp, pd)
    ps_all = ps_all.reshape(ps_all.shape[0], ts, pd)
    tm = PLE_TM
    nb = tp // tm
    blk = lambda i: jnp.minimum(i, nb - 1)
    prow = lambda width: pl.BlockSpec((tm, width), lambda i: (blk(i), 0))
    srow = lambda width: pl.BlockSpec((ts, width), lambda i: (0, 0))
    const = lambda r, c: pl.BlockSpec((None, r, c), lambda i: (layer, 0, 0), pipeline_mode=pl.Buffered(1))
    if emit_h:
        out_shape = tuple(jax.ShapeDtypeStruct((rows, d), dt) for rows in (tp, ts) for dt in (F32, BF16))
        out_specs = (prow(d), prow(d), srow(d), srow(d))
    else:
        out_shape = (jax.ShapeDtypeStruct((tp, d), F32), jax.ShapeDtypeStruct((ts, d), F32))
        out_specs = (prow(d), srow(d))
    return pl.pallas_call(
        functools.partial(_ple_kernel, emit_h=emit_h, n_blocks=nb),
        out_shape=out_shape,
        grid=(nb + 1,),
        in_specs=[prow(d), srow(d), const(d, d),
                  pl.BlockSpec((None, tm, pd), lambda i: (layer, blk(i), 0)),
                  pl.BlockSpec((None, ts, pd), lambda i: (layer, 0, 0)),
                  const(pd, d), prow(d), srow(d), pl.BlockSpec((1, d), lambda i: (0, 0))],
        out_specs=out_specs,
        scratch_shapes=[pltpu.VMEM((d, d), BF16), pltpu.VMEM((pd, d), BF16)],
        compiler_params=_params("arbitrary"),
        name="ple_residual",
    )(hnp, hns, wg_all, pp_all, ps_all, wp_all, hp, hs, g_next.reshape(1, d))


def _trunks(x_prompt, x_sample, state_ret, state_conv, p_prompt, p_sample, w):
    bp, lp = x_prompt.shape[0], x_prompt.shape[1]
    bs, ls = x_sample.shape[0], x_sample.shape[1]
    hp = x_prompt.reshape(bp * lp, D_MODEL)
    hs = x_sample.reshape(bs * ls, D_MODEL)
    hnp = rmsnorm_rows(hp, w["norm_mix_g"][0], BF16)
    hns = rmsnorm_rows(hs, w["norm_mix_g"][0], BF16)
    ret_p, conv_p, conv_s, v_s = [], [], [], []
    ret_s = None
    for i in range(DEPTH):
        li = i // N_MIXERS
        if i % N_MIXERS == 0:
            zp, zs = matmul_act(hnp, hns, w["gmlp_w_in"], li, "gelu", BF16, "gmlp_in")
            gate_w = (w["gmlp_ln_g"][li], w["gmlp_ln_b"][li], w["gmlp_w_s"][li], w["gmlp_b_s"][li])
            xmp = gmlp_gate_prompt(zp, *gate_w)
            xms, vn = gmlp_gate_sample(zs, *gate_w)
            v_s.append(vn.reshape(bs, ls, GMLP_WIDTH))
            w_out = w["gmlp_w_out"]
        else:
            qp, qs = matmul_act(hnp, hns, w["ret_w_in"], li, None, F32, "ret_in")
            xmp, s_new, xms, ret_s = retention(qp, qs, bp, lp, state_ret, li, ret_s)
            ret_p.append(s_new)
            w_out = w["ret_w_out"]
        hp, hnp, hs, hns = proj_residual(xmp, xms, w_out, li, hp, hs, w["norm_ffn_g"][i],
                                         MIXER_OUT_W_CHUNK, "mixer_out")
        yp, ys, cp, cs = ffn_in(hnp, hns, w["ffn_w_in"], i, w["ffn_cw"][i], w["ffn_cb"][i], state_conv[i], bp, lp)
        conv_p.append(cp)
        conv_s.append(cs)
        hp, hnp, hs, hns = proj_residual(yp, ys, w["ffn_w_down"], i, hp, hs, w["norm_ple_g"][i],
                                         FFN_DOWN_W_CHUNK, "ffn_down")
        last = i == DEPTH - 1
        g_next = w["final_norm_g"] if last else w["norm_mix_g"][i + 1]
        outs = ple_residual(hnp, hns, w["ple_w_gate"], p_prompt, p_sample, i, w["ple_w_proj"], hp, hs, g_next,
                            not last)
        if last:
            y_p, y_s = outs
        else:
            hp, hnp, hs, hns = outs
    return (y_p.reshape(bp, lp, D_MODEL), y_s.reshape(bs, ls, D_MODEL),
            jnp.stack(ret_p), ret_s.reshape(state_ret.shape),
            jnp.stack(conv_p), jnp.stack(conv_s), jnp.stack(v_s))


def kernel(x_prompt, x_sample, state_ret, state_conv, p_prompt, p_sample, norm_mix_g, norm_ffn_g, norm_ple_g, final_norm_g, gmlp_w_in, gmlp_ln_g, gmlp_ln_b, gmlp_w_s, gmlp_b_s, gmlp_w_out, ret_w_in, ret_w_out, ffn_w_in, ffn_conv_w, ffn_conv_b, ffn_w_down, ple_w_proj, ple_w_gate):
    ff_pad = D_FF_PAD - D_FF
    w = {
        "norm_mix_g": norm_mix_g, "norm_ffn_g": norm_ffn_g, "norm_ple_g": norm_ple_g, "final_norm_g": final_norm_g,
        "gmlp_w_in": gmlp_w_in, "gmlp_ln_g": gmlp_ln_g, "gmlp_ln_b": gmlp_ln_b,
        "gmlp_w_s": gmlp_w_s, "gmlp_b_s": gmlp_b_s, "gmlp_w_out": gmlp_w_out,
        "ret_w_in": ret_w_in, "ret_w_out": ret_w_out,
        "ffn_w_in": ffn_w_in,
        "ffn_cw": jnp.pad(ffn_conv_w, ((0, 0), (0, 0), (0, ff_pad))),
        "ffn_cb": jnp.pad(ffn_conv_b, ((0, 0), (0, ff_pad))).reshape(DEPTH, 1, D_FF_PAD),
        "ffn_w_down": ffn_w_down,
        "ple_w_proj": ple_w_proj, "ple_w_gate": ple_w_gate,
    }
    return _trunks(x_prompt, x_sample, state_ret, state_conv, p_prompt, p_sample, w)
```

```python
import functools
import math

import jax
import jax.numpy as jnp
from jax import lax
from jax.experimental import pallas as pl
from jax.experimental.pallas import tpu as pltpu

F32 = jnp.float32
BF16 = jnp.bfloat16

D_MODEL = 2048
DEPTH = 4
PAST_LEN = 16384
N_MIXERS = 2
PLE_DIM = 256
GMLP_CHUNK = 128
GMLP_WIDTH = 2 * D_MODEL
GMLP_GROUPS = 8
GMLP_GROUP_DIM = GMLP_WIDTH // GMLP_GROUPS
RET_HEADS = 8
RET_DK = D_MODEL // RET_HEADS
RET_DV = 2 * RET_DK
RET_HK = RET_HEADS * RET_DK
RET_HV = RET_HEADS * RET_DV
RET_CHUNK = 128
ROT_BASE = 10000.0
D_FF = ((8 * D_MODEL // 3 + 127) // 128) * 128
CONV_W = 3
EPS = 1e-6

V7X_VMEM_BYTES = 64 * 1024 * 1024
VMEM_LIMIT_BYTES = V7X_VMEM_BYTES - 8 * 1024 * 1024
SUBLANES = 8
LANES = 128
MXU_DEPTH = 256
D_FF_PAD = -(-D_FF // (2 * MXU_DEPTH)) * (2 * MXU_DEPTH)
FFN_TN = 512
FFN_ROW_CHUNK = 1024
MM_TM = 1024
MM_TN = 1024
PROJ_TM = 256
PLE_TM = 256
MIXER_OUT_W_CHUNK = 512
FFN_DOWN_W_CHUNK = 128


def _params(*sem):
    return pltpu.CompilerParams(dimension_semantics=sem, vmem_limit_bytes=VMEM_LIMIT_BYTES)


def _rms(x, g):
    r = lax.rsqrt(jnp.mean(x * x, axis=-1, keepdims=True) + EPS)
    return x * r * g


def _group_norm(o):
    mu = jnp.mean(o, axis=-1, keepdims=True)
    d = o - mu
    var = jnp.mean(d * d, axis=-1, keepdims=True)
    return d * lax.rsqrt(var + EPS)


def _rotate(x, cos, sin_even, sin_odd):
    n = x.shape[-1]
    return x * cos + pltpu.roll(x, n - 1, 1) * sin_even + pltpu.roll(x, 1, 1) * sin_odd


def _rms_kernel(x_ref, g_ref, o_ref):
    o_ref[...] = _rms(x_ref[...], g_ref[...]).astype(o_ref.dtype)


def rmsnorm_rows(x, g, out_dtype):
    t, d = x.shape
    tm = min(256, t)
    return pl.pallas_call(
        _rms_kernel,
        out_shape=jax.ShapeDtypeStruct((t, d), out_dtype),
        grid=(t // tm,),
        in_specs=[pl.BlockSpec((tm, d), lambda i: (i, 0)), pl.BlockSpec((1, d), lambda i: (0, 0))],
        out_specs=pl.BlockSpec((tm, d), lambda i: (i, 0)),
        compiler_params=_params("parallel"),
        name="rmsnorm_rows",
    )(x, g.reshape(1, d))


def _mm_kernel(xp_ref, xs_ref, w_ref, op_ref, os_ref, w_s, *, act):
    i = pl.program_id(1)

    def run(x_ref, o_ref):
        acc = jnp.dot(x_ref[...], w_s[...], preferred_element_type=F32)
        if act == "gelu":
            acc = jax.nn.gelu(acc)
        o_ref[...] = acc.astype(o_ref.dtype)

    @pl.when(i == 0)
    def _():
        w_s[...] = w_ref[...].astype(BF16)
        run(xs_ref, os_ref)

    @pl.when(i > 0)
    def _():
        run(xp_ref, op_ref)


def matmul_act(xp, xs, w_all, layer, act, prompt_dtype, name):
    tp, k = xp.shape
    ts = xs.shape[0]
    n = w_all.shape[2]
    tm = MM_TM
    tn = MM_TN
    nb = tp // tm
    blk = lambda i: jnp.maximum(i - 1, 0)
    return pl.pallas_call(
        functools.partial(_mm_kernel, act=act),
        out_shape=(jax.ShapeDtypeStruct((tp, n), prompt_dtype), jax.ShapeDtypeStruct((ts, n), F32)),
        grid=(n // tn, nb + 1),
        in_specs=[pl.BlockSpec((tm, k), lambda j, i: (blk(i), 0)),
                  pl.BlockSpec((ts, k), lambda j, i: (0, 0)),
                  pl.BlockSpec((None, k, tn), lambda j, i: (layer, 0, j))],
        out_specs=(pl.BlockSpec((tm, tn), lambda j, i: (blk(i), j)),
                   pl.BlockSpec((ts, tn), lambda j, i: (0, j))),
        scratch_shapes=[pltpu.VMEM((k, tn), BF16)],
        compiler_params=_params("arbitrary", "arbitrary"),
        name=name,
    )(xp, xs, w_all)


def _layernorm(v, g, b):
    mu = jnp.mean(v, axis=-1, keepdims=True)
    d = v - mu
    var = jnp.mean(d * d, axis=-1, keepdims=True)
    return d * lax.rsqrt(var + EPS) * g + b


GATE_CHUNKS_PER_STEP = 2


def _gmlp_gate_kernel(u_ref, v_ref, lng_ref, lnb_ref, ws_ref, bs_ref, x_ref):
    c = GMLP_CHUNK
    causal = lax.broadcasted_iota(jnp.int32, (c, c), 0) >= lax.broadcasted_iota(jnp.int32, (c, c), 1)
    for r in range(GATE_CHUNKS_PER_STEP):
        rows = slice(r * c, (r + 1) * c)
        vn = _layernorm(v_ref[rows, :].astype(F32), lng_ref[...], lnb_ref[...]).astype(BF16)
        for g in range(GMLP_GROUPS):
            cols = slice(g * GMLP_GROUP_DIM, (g + 1) * GMLP_GROUP_DIM)
            w = jnp.where(causal, ws_ref[g], 0.0).astype(BF16)
            s = jnp.dot(w, vn[:, cols], preferred_element_type=F32) + bs_ref[g]
            x_ref[rows, cols] = (u_ref[rows, cols].astype(F32) * s).astype(x_ref.dtype)


def gmlp_gate_prompt(z, ln_g, ln_b, w_s, b_s):
    t = z.shape[0]
    w = GMLP_WIDTH
    c = GMLP_CHUNK
    tm = c * GATE_CHUNKS_PER_STEP
    bs_b = jnp.broadcast_to(b_s[:, :, None], (GMLP_GROUPS, c, GMLP_GROUP_DIM))
    return pl.pallas_call(
        _gmlp_gate_kernel,
        out_shape=jax.ShapeDtypeStruct((t, w), BF16),
        grid=(t // tm,),
        in_specs=[
            pl.BlockSpec((tm, w), lambda i: (i, 0)),
            pl.BlockSpec((tm, w), lambda i: (i, 1)),
            pl.BlockSpec((1, w), lambda i: (0, 0)),
            pl.BlockSpec((1, w), lambda i: (0, 0)),
            pl.BlockSpec((GMLP_GROUPS, c, c), lambda i: (0, 0, 0)),
            pl.BlockSpec((GMLP_GROUPS, c, GMLP_GROUP_DIM), lambda i: (0, 0, 0)),
        ],
        out_specs=pl.BlockSpec((tm, w), lambda i: (i, 0)),
        compiler_params=_params("parallel"),
        name="gmlp_gate_prompt",
    )(z, z, ln_g.reshape(1, w), ln_b.reshape(1, w), w_s, bs_b)


def _gmlp_gate_sample_kernel(u_ref, v_ref, lng_ref, lnb_ref, scale_ref, shift_ref, x_ref, vn_ref):
    vn = _layernorm(v_ref[...], lng_ref[...], lnb_ref[...])
    vn_ref[...] = vn
    x_ref[...] = (u_ref[...] * (vn * scale_ref[...] + shift_ref[...])).astype(x_ref.dtype)


def gmlp_gate_sample(z, ln_g, ln_b, w_s, b_s):
    t = z.shape[0]
    w = GMLP_WIDTH
    scale = jnp.repeat(w_s[:, 0, 0], GMLP_GROUP_DIM).reshape(1, w)
    shift = jnp.repeat(b_s[:, 0], GMLP_GROUP_DIM).reshape(1, w)
    row = pl.BlockSpec((1, w), lambda i: (0, 0))
    return pl.pallas_call(
        _gmlp_gate_sample_kernel,
        out_shape=(jax.ShapeDtypeStruct((t, w), BF16), jax.ShapeDtypeStruct((t, w), F32)),
        grid=(1,),
        in_specs=[pl.BlockSpec((t, w), lambda i: (0, 0)), pl.BlockSpec((t, w), lambda i: (0, 1)), row, row, row, row],
        out_specs=(pl.BlockSpec((t, w), lambda i: (0, 0)), pl.BlockSpec((t, w), lambda i: (0, 0))),
        compiler_params=_params("arbitrary"),
        name="gmlp_gate_sample",
    )(z, z, ln_g.reshape(1, w), ln_b.reshape(1, w), scale, shift)


def _rotary_tables(pos):
    angle = 1.0 / (ROT_BASE ** jnp.linspace(0.0, 1.0, RET_DK // 2, dtype=F32))
    ph = pos.astype(F32)[:, None] * angle[None, :]
    cos = jnp.cos(ph)
    sin = jnp.sin(ph)
    zero = jnp.zeros_like(sin)
    n = pos.shape[0]
    cos_r = jnp.stack([cos, cos], axis=-1).reshape(n, RET_DK)
    sin_even = jnp.stack([-sin, zero], axis=-1).reshape(n, RET_DK)
    sin_odd = jnp.stack([zero, sin], axis=-1).reshape(n, RET_DK)
    return cos_r, sin_even, sin_odd


def _log_gamma():
    return jnp.log(1.0 - 2.0 ** (-5.0 - jnp.arange(RET_HEADS, dtype=F32)))


def _ret_prompt_heads(heads, cdec_ref, q_ref, k_ref, v_ref, g_ref, cos_ref, se_ref, so_ref,
                      din_ref, qd_ref, kd_ref, x_ref, s_ref):
    cos = cos_ref[...]
    se = se_ref[...]
    so = so_ref[...]
    for h in heads:
        kc = slice(h * RET_DK, (h + 1) * RET_DK)
        vc = slice(h * RET_DV, (h + 1) * RET_DV)
        q = _rotate(q_ref[:, kc], cos, se, so)
        k = _rotate(k_ref[:, kc], cos, se, so) * (RET_DK ** -0.5)
        v = v_ref[:, vc].astype(BF16)
        scores = lax.dot_general(q.astype(BF16), k.astype(BF16), (((1,), (1,)), ((), ())),
                                 preferred_element_type=F32) * din_ref[h]
        s_old = s_ref[0, h]
        lhs = jnp.concatenate([scores.astype(BF16), (q * qd_ref[h]).astype(BF16)], axis=1)
        rhs = jnp.concatenate([v, s_old.astype(BF16)], axis=0)
        o = jnp.dot(lhs, rhs, preferred_element_type=F32)
        kt = jnp.transpose(k * kd_ref[h]).astype(BF16)
        s_ref[0, h] = cdec_ref[h] * s_old + jnp.dot(kt, v, preferred_element_type=F32)
        on = _group_norm(o)
        x_ref[:, vc] = (jax.nn.silu(g_ref[:, vc]) * on).astype(x_ref.dtype)


def _ret_sample_token(cdec_ref, row_ref, cos_ref, se_ref, so_ref, gq_ref, s_ref, x_ref, so_out_ref):
    row = row_ref[0]
    q = _rotate(row[:, 0:RET_HK], cos_ref[...], se_ref[...], so_ref[...])
    k = _rotate(row[:, RET_HK:2 * RET_HK], cos_ref[...], se_ref[...], so_ref[...]) * (RET_DK ** -0.5)
    v = row[:, 2 * RET_HK:2 * RET_HK + RET_HV]
    g = row[:, 2 * RET_HK + RET_HV:]
    qk = q * k

    pad = LANES
    head_of_row = lax.broadcasted_iota(jnp.int32, (pad, RET_HK), 0)
    head_of_col = lax.shift_right_logical(lax.broadcasted_iota(jnp.int32, (pad, RET_HK), 1),
                                          int(math.log2(RET_DK)))
    on_diag = head_of_row == head_of_col
    q_bd = jnp.where(on_diag[:SUBLANES], q * gq_ref[...], 0.0).astype(BF16)
    k_bd = jnp.where(on_diag, k, 0.0)
    s_old = s_ref[0]
    cross = jnp.dot(q_bd, s_old.astype(BF16), preferred_element_type=F32)
    v_row_id = lax.broadcasted_iota(jnp.int32, (pad, RET_DV), 0)
    v_rows = jnp.zeros((pad, RET_DV), F32)
    for h in range(RET_HEADS):
        v_rows = jnp.where(v_row_id == h, v[:, h * RET_DV:(h + 1) * RET_DV], v_rows)
    k_cols = jnp.transpose(k_bd).astype(BF16)
    v_rows = v_rows.astype(BF16)
    for h in range(RET_HEADS):
        kc = slice(h * RET_DK, (h + 1) * RET_DK)
        vc = slice(h * RET_DV, (h + 1) * RET_DV)
        score = jnp.sum(qk[:, kc], axis=-1, keepdims=True)
        on = _group_norm(score * v[:, vc] + cross[h:h + 1, :])
        x_ref[0, :, vc] = (jax.nn.silu(g[:, vc]) * on).astype(x_ref.dtype)
        outer = jnp.dot(k_cols[kc, :], v_rows, preferred_element_type=F32)
        so_out_ref[0, kc, :] = cdec_ref[h] * s_old[kc, :] + outer


N_SAMPLE_IN_REFS = 7


def _ret_prompt_kernel(*refs):
    s_ref = refs[-1]

    @pl.when(pl.program_id(1) == 0)
    def _():
        s_ref[...] = jnp.zeros_like(s_ref)

    _ret_prompt_heads(range(RET_HEADS), *refs)


def _ret_sample_kernel(*refs):
    _ret_sample_token(*refs[:N_SAMPLE_IN_REFS], *refs[-2:])


def retention(qkvg_p, qkvg_s, batch, seq, state_all, li, new_states):
    c = RET_CHUNK
    n = seq // c
    bs, n_in = qkvg_s.shape
    n_layers = state_all.shape[0]
    lg = _log_gamma()
    idx = jnp.arange(c, dtype=F32)
    diff = idx[:, None] - idx[None, :]
    decay_in = jnp.where(diff[None] >= 0, jnp.exp(lg[:, None, None] * jnp.maximum(diff, 0.0)[None]), 0.0)
    q_dec = jnp.exp(lg[:, None] * (idx + 1.0)[None, :])
    k_dec = jnp.exp(lg[:, None] * (c - 1.0 - idx)[None, :])
    chunk_dec = jnp.exp(lg * c)
    qd = jnp.broadcast_to(q_dec[:, :, None], (RET_HEADS, c, RET_DK))
    kd = jnp.broadcast_to(k_dec[:, :, None], (RET_HEADS, c, RET_DK))
    cos_p, se_p, so_p = _rotary_tables(jnp.arange(seq, dtype=jnp.int32))
    gamma = jnp.exp(lg * 1.0)
    gq = jnp.repeat(gamma, RET_DK).reshape(1, RET_HK)
    cos_s, se_s, so_s = (jnp.tile(t, (1, RET_HEADS))
                         for t in _rotary_tables(PAST_LEN + jnp.arange(1, dtype=jnp.int32)))

    smem = pl.BlockSpec(memory_space=pltpu.SMEM)

    rows = lambda b, j: b * n + j
    tab = pl.BlockSpec((c, RET_DK), lambda b, j: (j, 0))
    full3 = lambda shape: pl.BlockSpec(shape, lambda b, j: (0, 0, 0))
    xp, sp = pl.pallas_call(
        _ret_prompt_kernel,
        out_shape=(jax.ShapeDtypeStruct((batch * seq, RET_HV), BF16),
                   jax.ShapeDtypeStruct((batch, RET_HEADS, RET_DK, RET_DV), F32)),
        grid=(batch, n),
        in_specs=[
            smem,
            pl.BlockSpec((c, RET_HK), lambda b, j: (rows(b, j), 0)),
            pl.BlockSpec((c, RET_HK), lambda b, j: (rows(b, j), 1)),
            pl.BlockSpec((c, RET_HV), lambda b, j: (rows(b, j), 1)),
            pl.BlockSpec((c, RET_HV), lambda b, j: (rows(b, j), 2)),
            tab, tab, tab,
            full3((RET_HEADS, c, c)), full3((RET_HEADS, c, RET_DK)), full3((RET_HEADS, c, RET_DK)),
        ],
        out_specs=(pl.BlockSpec((c, RET_HV), lambda b, j: (rows(b, j), 0)),
                   pl.BlockSpec((1, RET_HEADS, RET_DK, RET_DV), lambda b, j: (b, 0, 0, 0))),
        compiler_params=_params("parallel", "arbitrary"),
        name="retention_prompt",
    )(chunk_dec, qkvg_p, qkvg_p, qkvg_p, qkvg_p, cos_p, se_p, so_p, decay_in, qd, kd)

    vec = pl.BlockSpec((1, RET_HK), lambda i: (0, 0))
    state_spec = pl.BlockSpec((None, 1, RET_HK, RET_DV), lambda i: (li, i, 0, 0))
    in_specs = [smem, pl.BlockSpec((1, 1, n_in), lambda i: (i, 0, 0)), vec, vec, vec, vec, state_spec]
    args = [gamma, qkvg_s.reshape(bs, 1, n_in), cos_s, se_s, so_s, gq,
            state_all.reshape(n_layers, bs, RET_HK, RET_DV)]
    assert len(args) == N_SAMPLE_IN_REFS
    aliases = {}
    if new_states is not None:
        aliases = {len(args): 1}
        in_specs.append(pl.BlockSpec(memory_space=pl.ANY))
        args.append(new_states)
    xs, ss = pl.pallas_call(
        _ret_sample_kernel,
        out_shape=(jax.ShapeDtypeStruct((bs, 1, RET_HV), BF16),
                   jax.ShapeDtypeStruct((n_layers, bs, RET_HK, RET_DV), F32)),
        grid=(bs,),
        in_specs=in_specs,
        out_specs=(pl.BlockSpec((1, 1, RET_HV), lambda i: (i, 0, 0)), state_spec),
        input_output_aliases=aliases,
        compiler_params=_params("parallel"),
        name="retention_sample",
    )(*args)
    return xp, sp, xs.reshape(bs, RET_HV), ss


def _proj_res_kernel(xp_ref, xs_ref, w_ref, hp_ref, hs_ref, g_ref, hop_ref, hnp_ref, hos_ref, hns_ref, w_s,
                     *, k_rows, n_chunks, n_blocks):
    step = pl.program_id(0)
    chunk = w_ref.shape[0]

    @pl.when(step < n_chunks)
    def _():
        if w_s.shape[0] > k_rows:
            @pl.when(step == 0)
            def _():
                w_s[k_rows:, :] = jnp.zeros((w_s.shape[0] - k_rows, w_s.shape[1]), BF16)

        start = pl.multiple_of(step * chunk, 2 * SUBLANES)
        w_s[pl.ds(start, chunk), :] = w_ref[...].astype(BF16)

    def run(x_ref, h_ref, ho_ref, hn_ref):
        h_new = h_ref[...] + jnp.dot(x_ref[...], w_s[...], preferred_element_type=F32)
        ho_ref[...] = h_new
        hn_ref[...] = _rms(h_new, g_ref[...]).astype(hn_ref.dtype)

    @pl.when(jnp.logical_and(step >= n_chunks, step < n_chunks + n_blocks))
    def _():
        run(xp_ref, hp_ref, hop_ref, hnp_ref)

    @pl.when(step == n_chunks + n_blocks)
    def _():
        run(xs_ref, hs_ref, hos_ref, hns_ref)


def proj_residual(xp, xs, w_all, layer, hp, hs, g_next, chunk, name):
    tp, k = xp.shape
    ts = xs.shape[0]
    k_rows, d = w_all.shape[1], w_all.shape[2]
    n_chunks = k_rows // chunk
    assert chunk * n_chunks == k_rows and chunk % (2 * SUBLANES) == 0
    tm = PROJ_TM
    nb = tp // tm
    blk = lambda s: (jnp.clip(s - n_chunks, 0, nb - 1), 0)
    fixed = lambda s: (0, 0)
    f32 = lambda rows: jax.ShapeDtypeStruct((rows, d), F32)
    bf16 = lambda rows: jax.ShapeDtypeStruct((rows, d), BF16)
    return pl.pallas_call(
        functools.partial(_proj_res_kernel, k_rows=k_rows, n_chunks=n_chunks, n_blocks=nb),
        out_shape=(f32(tp), bf16(tp), f32(ts), bf16(ts)),
        grid=(n_chunks + nb + 1,),
        in_specs=[
            pl.BlockSpec((tm, k), blk),
            pl.BlockSpec((ts, k), fixed),
            pl.BlockSpec((None, chunk, d), lambda s: (layer, jnp.minimum(s, n_chunks - 1), 0)),
            pl.BlockSpec((tm, d), blk),
            pl.BlockSpec((ts, d), fixed),
            pl.BlockSpec((1, d), fixed),
        ],
        out_specs=(pl.BlockSpec((tm, d), blk), pl.BlockSpec((tm, d), blk),
                   pl.BlockSpec((ts, d), fixed), pl.BlockSpec((ts, d), fixed)),
        scratch_shapes=[pltpu.VMEM((k, d), BF16)],
        compiler_params=_params("arbitrary"),
        name=name,
    )(xp, xs, w_all, hp, hs, g_next.reshape(1, d))


def _conv_gate(a, a1, a2, b, cw_ref, cb_ref):
    conv = cb_ref[...] + cw_ref[0:1, :] * a2 + cw_ref[1:2, :] * a1 + cw_ref[2:3, :] * a
    return jax.nn.gelu(conv) * b


FFN_W_BLOCKS = FFN_TN // LANES
FFN_COL_BLOCKS = D_FF // LANES


def _ffn_weight_specs(layer, d, col_tile_of):
    last = 2 * FFN_COL_BLOCKS - 1

    def spec(first_block, r):
        def index(*grid_idx):
            return (layer, 0, jnp.minimum(first_block + FFN_W_BLOCKS * col_tile_of(*grid_idx) + r, last))
        return pl.BlockSpec((None, d, LANES), index)

    return ([spec(0, r) for r in range(FFN_W_BLOCKS)]
            + [spec(FFN_COL_BLOCKS, r) for r in range(FFN_W_BLOCKS)])


def _cast_weight_blocks(block_refs, dst):
    for r, ref in enumerate(block_refs):
        dst[:, r * LANES:(r + 1) * LANES] = ref[...].astype(BF16)


def _valid_cols(col_tile, shape):
    return col_tile * FFN_TN + lax.broadcasted_iota(jnp.int32, shape, 1) < D_FF


def _ffn_in_kernel(xp_ref, xs_ref, *refs):
    wa_refs, wb_refs = refs[:FFN_W_BLOCKS], refs[FFN_W_BLOCKS:2 * FFN_W_BLOCKS]
    cw_ref, cb_ref, b0_ref, b1_ref, yp_ref, st_ref, ys_ref, as_ref, wa_s, wb_s = refs[2 * FFN_W_BLOCKS:]
    col_tile = pl.program_id(0)
    b = pl.program_id(1)

    @pl.when(b == 0)
    def _():
        _cast_weight_blocks(wa_refs, wa_s)
        _cast_weight_blocks(wb_refs, wb_s)
        x = xs_ref[...]
        a = jnp.dot(x, wa_s[...], preferred_element_type=F32)
        g = jnp.dot(x, wb_s[...], preferred_element_type=F32)
        as_ref[...] = a
        y = jnp.where(_valid_cols(col_tile, a.shape),
                      _conv_gate(a, b1_ref[...], b0_ref[...], g, cw_ref, cb_ref), 0.0)
        ys_ref[...] = y.astype(ys_ref.dtype)

    @pl.when(b > 0)
    def _():
        seq = xp_ref.shape[0]
        rc = FFN_ROW_CHUNK
        tn = FFN_TN
        rows = lax.broadcasted_iota(jnp.int32, (rc, tn), 0)
        valid = _valid_cols(col_tile, (rc, tn))
        prev1 = jnp.zeros((1, tn), F32)
        prev2 = jnp.zeros((1, tn), F32)
        for r in range(seq // rc):
            x = xp_ref[r * rc:(r + 1) * rc, :]
            a = jnp.dot(x, wa_s[...], preferred_element_type=F32)
            g = jnp.dot(x, wb_s[...], preferred_element_type=F32)
            a1 = jnp.where(rows >= 1, pltpu.roll(a, 1, 0), prev1)
            a2 = jnp.where(rows >= 2, pltpu.roll(a, 2, 0), jnp.where(rows == 1, prev1, prev2))
            y = jnp.where(valid, _conv_gate(a, a1, a2, g, cw_ref, cb_ref), 0.0)
            yp_ref[r * rc:(r + 1) * rc, :] = y.astype(yp_ref.dtype)
            prev1 = a[rc - 1:rc, :]
            prev2 = a[rc - 2:rc - 1, :]
            if r == seq // rc - 1:
                st_ref[0] = a[rc - SUBLANES:, :]


def ffn_in(hnp, hns, w_in_all, layer, cw, cb, buf, batch, seq):
    tn = FFN_TN
    d = hnp.shape[1]
    ts = hns.shape[0]
    pad = ((0, 0), (0, D_FF_PAD - D_FF))
    b0 = jnp.pad(buf[:, 0, :], pad)
    b1 = jnp.pad(buf[:, 1, :], pad)
    sq = lambda b: jnp.maximum(b - 1, 0)
    col = lambda rows_: pl.BlockSpec((rows_, tn), lambda j, b: (0, j))
    yp, st, ys, a_s = pl.pallas_call(
        _ffn_in_kernel,
        out_shape=(jax.ShapeDtypeStruct((batch * seq, D_FF_PAD), BF16),
                   jax.ShapeDtypeStruct((batch, SUBLANES, D_FF_PAD), F32),
                   jax.ShapeDtypeStruct((ts, D_FF_PAD), BF16),
                   jax.ShapeDtypeStruct((ts, D_FF_PAD), F32)),
        grid=(D_FF_PAD // tn, batch + 1),
        in_specs=[pl.BlockSpec((seq, d), lambda j, b: (sq(b), 0)), pl.BlockSpec((ts, d), lambda j, b: (0, 0))]
        + _ffn_weight_specs(layer, d, lambda j, b: j)
        + [col(CONV_W), col(1), col(ts), col(ts)],
        out_specs=(pl.BlockSpec((seq, tn), lambda j, b: (sq(b), j)),
                   pl.BlockSpec((1, SUBLANES, tn), lambda j, b: (sq(b), 0, j)),
                   col(ts), col(ts)),
        scratch_shapes=[pltpu.VMEM((d, tn), BF16), pltpu.VMEM((d, tn), BF16)],
        compiler_params=_params("arbitrary", "arbitrary"),
        name="ffn_in",
    )(hnp, hns, *([w_in_all] * (2 * FFN_W_BLOCKS)), cw, cb, b0, b1)
    conv_p = st[:, SUBLANES - (CONV_W - 1):, :D_FF]
    conv_s = jnp.stack([buf[:, 1, :], a_s[:, :D_FF]], axis=1)
    return yp, ys, conv_p, conv_s


def _ple_kernel(hnp_ref, hns_ref, wg_ref, pp_ref, ps_ref, wp_ref, hp_ref, hs_ref, g_ref, *refs, emit_h, n_blocks):
    out_refs, (wg_s, wp_s) = refs[:-2], refs[-2:]
    n_out = len(out_refs) // 2
    i = pl.program_id(0)

    @pl.when(i == 0)
    def _():
        wg_s[...] = wg_ref[...].astype(BF16)
        wp_s[...] = wp_ref[...].astype(BF16)

    def run(hn_ref, p_ref, h_ref, outs):
        gate = jax.nn.sigmoid(jnp.dot(hn_ref[...], wg_s[...], preferred_element_type=F32))
        proj = jnp.dot(p_ref[...].astype(BF16), wp_s[...], preferred_element_type=F32)
        h_new = h_ref[...] + gate * proj
        if emit_h:
            outs[0][...] = h_new
        outs[-1][...] = _rms(h_new, g_ref[...]).astype(outs[-1].dtype)

    @pl.when(i < n_blocks)
    def _():
        run(hnp_ref, pp_ref, hp_ref, out_refs[:n_out])

    @pl.when(i == n_blocks)
    def _():
        run(hns_ref, ps_ref, hs_ref, out_refs[n_out:])


def ple_residual(hnp, hns, wg_all, pp_all, ps_all, layer, wp_all, hp, hs, g_next, emit_h):
    tp, d = hp.shape
    ts = hs.shape[0]
    pd = pp_all.shape[-1]
    pp_all = pp_all.reshape(pp_all.shape[0], tp, pd)
    ps_all = ps_all.reshape(ps_all.shape[0], ts, pd)
    tm = PLE_TM
    nb = tp // tm
    blk = lambda i: jnp.minimum(i, nb - 1)
    prow = lambda width: pl.BlockSpec((tm, width), lambda i: (blk(i), 0))
    srow = lambda width: pl.BlockSpec((ts, width), lambda i: (0, 0))
    const = lambda r, c: pl.BlockSpec((None, r, c), lambda i: (layer, 0, 0), pipeline_mode=pl.Buffered(1))
    if emit_h:
        out_shape = tuple(jax.ShapeDtypeStruct((rows, d), dt) for rows in (tp, ts) for dt in (F32, BF16))
        out_specs = (prow(d), prow(d), srow(d), srow(d))
    else:
        out_shape = (jax.ShapeDtypeStruct((tp, d), F32), jax.ShapeDtypeStruct((ts, d), F32))
        out_specs = (prow(d), srow(d))
    return pl.pallas_call(
        functools.partial(_ple_kernel, emit_h=emit_h, n_blocks=nb),
        out_shape=out_shape,
        grid=(nb + 1,),
        in_specs=[prow(d), srow(d), const(d, d),
                  pl.BlockSpec((None, tm, pd), lambda i: (layer, blk(i), 0)),
                  pl.BlockSpec((None, ts, pd), lambda i: (layer, 0, 0)),
                  const(pd, d), prow(d), srow(d), pl.BlockSpec((1, d), lambda i: (0, 0))],
        out_specs=out_specs,
        scratch_shapes=[pltpu.VMEM((d, d), BF16), pltpu.VMEM((pd, d), BF16)],
        compiler_params=_params("arbitrary"),
        name="ple_residual",
    )(hnp, hns, wg_all, pp_all, ps_all, wp_all, hp, hs, g_next.reshape(1, d))


def _trunks(x_prompt, x_sample, state_ret, state_conv, p_prompt, p_sample, w):
    bp, lp = x_prompt.shape[0], x_prompt.shape[1]
    bs, ls = x_sample.shape[0], x_sample.shape[1]
    hp = x_prompt.reshape(bp * lp, D_MODEL)
    hs = x_sample.reshape(bs * ls, D_MODEL)
    hnp = rmsnorm_rows(hp, w["norm_mix_g"][0], BF16)
    hns = rmsnorm_rows(hs, w["norm_mix_g"][0], BF16)
    ret_p, conv_p, conv_s, v_s = [], [], [], []
    ret_s = None
    for i in range(DEPTH):
        li = i // N_MIXERS
        if i % N_MIXERS == 0:
            zp, zs = matmul_act(hnp, hns, w["gmlp_w_in"], li, "gelu", BF16, "gmlp_in")
            gate_w = (w["gmlp_ln_g"][li], w["gmlp_ln_b"][li], w["gmlp_w_s"][li], w["gmlp_b_s"][li])
            xmp = gmlp_gate_prompt(zp, *gate_w)
            xms, vn = gmlp_gate_sample(zs, *gate_w)
            v_s.append(vn.reshape(bs, ls, GMLP_WIDTH))
            w_out = w["gmlp_w_out"]
        else:
            qp, qs = matmul_act(hnp, hns, w["ret_w_in"], li, None, F32, "ret_in")
            xmp, s_new, xms, ret_s = retention(qp, qs, bp, lp, state_ret, li, ret_s)
            ret_p.append(s_new)
            w_out = w["ret_w_out"]
        hp, hnp, hs, hns = proj_residual(xmp, xms, w_out, li, hp, hs, w["norm_ffn_g"][i],
                                         MIXER_OUT_W_CHUNK, "mixer_out")
        yp, ys, cp, cs = ffn_in(hnp, hns, w["ffn_w_in"], i, w["ffn_cw"][i], w["ffn_cb"][i], state_conv[i], bp, lp)
        conv_p.append(cp)
        conv_s.append(cs)
        hp, hnp, hs, hns = proj_residual(yp, ys, w["ffn_w_down"], i, hp, hs, w["norm_ple_g"][i],
                                         FFN_DOWN_W_CHUNK, "ffn_down")
        last = i == DEPTH - 1
        g_next = w["final_norm_g"] if last else w["norm_mix_g"][i + 1]
        outs = ple_residual(hnp, hns, w["ple_w_gate"], p_prompt, p_sample, i, w["ple_w_proj"], hp, hs, g_next,
                            not last)
        if last:
            y_p, y_s = outs
        else:
            hp, hnp, hs, hns = outs
    return (y_p.reshape(bp, lp, D_MODEL), y_s.reshape(bs, ls, D_MODEL),
            jnp.stack(ret_p), ret_s.reshape(state_ret.shape),
            jnp.stack(conv_p), jnp.stack(conv_s), jnp.stack(v_s))


def kernel(x_prompt, x_sample, state_ret, state_conv, p_prompt, p_sample, norm_mix_g, norm_ffn_g, norm_ple_g, final_norm_g, gmlp_w_in, gmlp_ln_g, gmlp_ln_b, gmlp_w_s, gmlp_b_s, gmlp_w_out, ret_w_in, ret_w_out, ffn_w_in, ffn_conv_w, ffn_conv_b, ffn_w_down, ple_w_proj, ple_w_gate):
    ff_pad = D_FF_PAD - D_FF
    w = {
        "norm_mix_g": norm_mix_g, "norm_ffn_g": norm_ffn_g, "norm_ple_g": norm_ple_g, "final_norm_g": final_norm_g,
        "gmlp_w_in": gmlp_w_in, "gmlp_ln_g": gmlp_ln_g, "gmlp_ln_b": gmlp_ln_b,
        "gmlp_w_s": gmlp_w_s, "gmlp_b_s": gmlp_b_s, "gmlp_w_out": gmlp_w_out,
        "ret_w_in": ret_w_in, "ret_w_out": ret_w_out,
        "ffn_w_in": ffn_w_in,
        "ffn_cw": jnp.pad(ffn_conv_w, ((0, 0), (0, 0), (0, ff_pad))),
        "ffn_cb": jnp.pad(ffn_conv_b, ((0, 0), (0, ff_pad))).reshape(DEPTH, 1, D_FF_PAD),
        "ffn_w_down": ffn_w_down,
        "ple_w_proj": ple_w_proj, "ple_w_gate": ple_w_gate,
    }
    return _trunks(x_prompt, x_sample, state_ret, state_conv, p_prompt, p_sample, w)
```

```python
import functools
import math

import jax
import jax.numpy as jnp
from jax import lax
from jax.experimental import pallas as pl
from jax.experimental.pallas import tpu as pltpu

F32 = jnp.float32
BF16 = jnp.bfloat16

D_MODEL = 2048
DEPTH = 4
PAST_LEN = 16384
N_MIXERS = 2
PLE_DIM = 256
GMLP_CHUNK = 128
GMLP_WIDTH = 2 * D_MODEL
GMLP_GROUPS = 8
GMLP_GROUP_DIM = GMLP_WIDTH // GMLP_GROUPS
RET_HEADS = 8
RET_DK = D_MODEL // RET_HEADS
RET_DV = 2 * RET_DK
RET_HK = RET_HEADS * RET_DK
RET_HV = RET_HEADS * RET_DV
RET_CHUNK = 128
ROT_BASE = 10000.0
D_FF = ((8 * D_MODEL // 3 + 127) // 128) * 128
CONV_W = 3
EPS = 1e-6

V7X_VMEM_BYTES = 64 * 1024 * 1024
VMEM_LIMIT_BYTES = V7X_VMEM_BYTES - 8 * 1024 * 1024
SUBLANES = 8
LANES = 128
MXU_DEPTH = 256
D_FF_PAD = -(-D_FF // (2 * MXU_DEPTH)) * (2 * MXU_DEPTH)
FFN_TN = 512
FFN_ROW_CHUNK = 1024
MM_TM = 1024
MM_TN = 1024
PROJ_TM = 256
PLE_TM = 256
MIXER_OUT_W_CHUNK = 512
FFN_DOWN_W_CHUNK = 128


def _params(*sem):
    return pltpu.CompilerParams(dimension_semantics=sem, vmem_limit_bytes=VMEM_LIMIT_BYTES)


def _rms(x, g):
    r = lax.rsqrt(jnp.mean(x * x, axis=-1, keepdims=True) + EPS)
    return x * r * g


def _group_norm(o):
    mu = jnp.mean(o, axis=-1, keepdims=True)
    d = o - mu
    var = jnp.mean(d * d, axis=-1, keepdims=True)
    return d * lax.rsqrt(var + EPS)


def _rotate(x, cos, sin_even, sin_odd):
    n = x.shape[-1]
    return x * cos + pltpu.roll(x, n - 1, 1) * sin_even + pltpu.roll(x, 1, 1) * sin_odd


def _rms_kernel(x_ref, g_ref, o_ref):
    o_ref[...] = _rms(x_ref[...], g_ref[...]).astype(o_ref.dtype)


def rmsnorm_rows(x, g, out_dtype):
    t, d = x.shape
    tm = min(256, t)
    return pl.pallas_call(
        _rms_kernel,
        out_shape=jax.ShapeDtypeStruct((t, d), out_dtype),
        grid=(t // tm,),
        in_specs=[pl.BlockSpec((tm, d), lambda i: (i, 0)), pl.BlockSpec((1, d), lambda i: (0, 0))],
        out_specs=pl.BlockSpec((tm, d), lambda i: (i, 0)),
        compiler_params=_params("parallel"),
        name="rmsnorm_rows",
    )(x, g.reshape(1, d))


def _mm_kernel(xp_ref, xs_ref, w_ref, op_ref, os_ref, w_s, *, act):
    i = pl.program_id(1)

    def run(x_ref, o_ref):
        acc = jnp.dot(x_ref[...], w_s[...], preferred_element_type=F32)
        if act == "gelu":
            acc = jax.nn.gelu(acc)
        o_ref[...] = acc.astype(o_ref.dtype)

    @pl.when(i == 0)
    def _():
        w_s[...] = w_ref[...].astype(BF16)
        run(xs_ref, os_ref)

    @pl.when(i > 0)
    def _():
        run(xp_ref, op_ref)


def matmul_act(xp, xs, w_all, layer, act, prompt_dtype, name):
    tp, k = xp.shape
    ts = xs.shape[0]
    n = w_all.shape[2]
    tm = MM_TM
    tn = MM_TN
    nb = tp // tm
    blk = lambda i: jnp.maximum(i - 1, 0)
    return pl.pallas_call(
        functools.partial(_mm_kernel, act=act),
        out_shape=(jax.ShapeDtypeStruct((tp, n), prompt_dtype), jax.ShapeDtypeStruct((ts, n), F32)),
        grid=(n // tn, nb + 1),
        in_specs=[pl.BlockSpec((tm, k), lambda j, i: (blk(i), 0)),
                  pl.BlockSpec((ts, k), lambda j, i: (0, 0)),
                  pl.BlockSpec((None, k, tn), lambda j, i: (layer, 0, j))],
        out_specs=(pl.BlockSpec((tm, tn), lambda j, i: (blk(i), j)),
                   pl.BlockSpec((ts, tn), lambda j, i: (0, j))),
        scratch_shapes=[pltpu.VMEM((k, tn), BF16)],
        compiler_params=_params("arbitrary", "arbitrary"),
        name=name,
    )(xp, xs, w_all)


def _layernorm(v, g, b):
    mu = jnp.mean(v, axis=-1, keepdims=True)
    d = v - mu
    var = jnp.mean(d * d, axis=-1, keepdims=True)
    return d * lax.rsqrt(var + EPS) * g + b


GATE_CHUNKS_PER_STEP = 2


def _gmlp_gate_kernel(u_ref, v_ref, lng_ref, lnb_ref, ws_ref, bs_ref, x_ref):
    c = GMLP_CHUNK
    causal = lax.broadcasted_iota(jnp.int32, (c, c), 0) >= lax.broadcasted_iota(jnp.int32, (c, c), 1)
    for r in range(GATE_CHUNKS_PER_STEP):
        rows = slice(r * c, (r + 1) * c)
        vn = _layernorm(v_ref[rows, :].astype(F32), lng_ref[...], lnb_ref[...]).astype(BF16)
        for g in range(GMLP_GROUPS):
            cols = slice(g * GMLP_GROUP_DIM, (g + 1) * GMLP_GROUP_DIM)
            w = jnp.where(causal, ws_ref[g], 0.0).astype(BF16)
            s = jnp.dot(w, vn[:, cols], preferred_element_type=F32) + bs_ref[g]
            x_ref[rows, cols] = (u_ref[rows, cols].astype(F32) * s).astype(x_ref.dtype)


def gmlp_gate_prompt(z, ln_g, ln_b, w_s, b_s):
    t = z.shape[0]
    w = GMLP_WIDTH
    c = GMLP_CHUNK
    tm = c * GATE_CHUNKS_PER_STEP
    bs_b = jnp.broadcast_to(b_s[:, :, None], (GMLP_GROUPS, c, GMLP_GROUP_DIM))
    return pl.pallas_call(
        _gmlp_gate_kernel,
        out_shape=jax.ShapeDtypeStruct((t, w), BF16),
        grid=(t // tm,),
        in_specs=[
            pl.BlockSpec((tm, w), lambda i: (i, 0)),
            pl.BlockSpec((tm, w), lambda i: (i, 1)),
            pl.BlockSpec((1, w), lambda i: (0, 0)),
            pl.BlockSpec((1, w), lambda i: (0, 0)),
            pl.BlockSpec((GMLP_GROUPS, c, c), lambda i: (0, 0, 0)),
            pl.BlockSpec((GMLP_GROUPS, c, GMLP_GROUP_DIM), lambda i: (0, 0, 0)),
        ],
        out_specs=pl.BlockSpec((tm, w), lambda i: (i, 0)),
        compiler_params=_params("parallel"),
        name="gmlp_gate_prompt",
    )(z, z, ln_g.reshape(1, w), ln_b.reshape(1, w), w_s, bs_b)


def _gmlp_gate_sample_kernel(u_ref, v_ref, lng_ref, lnb_ref, scale_ref, shift_ref, x_ref, vn_ref):
    vn = _layernorm(v_ref[...], lng_ref[...], lnb_ref[...])
    vn_ref[...] = vn
    x_ref[...] = (u_ref[...] * (vn * scale_ref[...] + shift_ref[...])).astype(x_ref.dtype)


def gmlp_gate_sample(z, ln_g, ln_b, w_s, b_s):
    t = z.shape[0]
    w = GMLP_WIDTH
    scale = jnp.repeat(w_s[:, 0, 0], GMLP_GROUP_DIM).reshape(1, w)
    shift = jnp.repeat(b_s[:, 0], GMLP_GROUP_DIM).reshape(1, w)
    row = pl.BlockSpec((1, w), lambda i: (0, 0))
    return pl.pallas_call(
        _gmlp_gate_sample_kernel,
        out_shape=(jax.ShapeDtypeStruct((t, w), BF16), jax.ShapeDtypeStruct((t, w), F32)),
        grid=(1,),
        in_specs=[pl.BlockSpec((t, w), lambda i: (0, 0)), pl.BlockSpec((t, w), lambda i: (0, 1)), row, row, row, row],
        out_specs=(pl.BlockSpec((t, w), lambda i: (0, 0)), pl.BlockSpec((t, w), lambda i: (0, 0))),
        compiler_params=_params("arbitrary"),
        name="gmlp_gate_sample",
    )(z, z, ln_g.reshape(1, w), ln_b.reshape(1, w), scale, shift)


def _rotary_tables(pos):
    angle = 1.0 / (ROT_BASE ** jnp.linspace(0.0, 1.0, RET_DK // 2, dtype=F32))
    ph = pos.astype(F32)[:, None] * angle[None, :]
    cos = jnp.cos(ph)
    sin = jnp.sin(ph)
    zero = jnp.zeros_like(sin)
    n = pos.shape[0]
    cos_r = jnp.stack([cos, cos], axis=-1).reshape(n, RET_DK)
    sin_even = jnp.stack([-sin, zero], axis=-1).reshape(n, RET_DK)
    sin_odd = jnp.stack([zero, sin], axis=-1).reshape(n, RET_DK)
    return cos_r, sin_even, sin_odd


def _log_gamma():
    return jnp.log(1.0 - 2.0 ** (-5.0 - jnp.arange(RET_HEADS, dtype=F32)))


def _ret_prompt_heads(heads, cdec_ref, q_ref, k_ref, v_ref, g_ref, cos_ref, se_ref, so_ref,
                      din_ref, qd_ref, kd_ref, x_ref, s_ref):
    cos = cos_ref[...]
    se = se_ref[...]
    so = so_ref[...]
    for h in heads:
        kc = slice(h * RET_DK, (h + 1) * RET_DK)
        vc = slice(h * RET_DV, (h + 1) * RET_DV)
        q = _rotate(q_ref[:, kc], cos, se, so)
        k = _rotate(k_ref[:, kc], cos, se, so) * (RET_DK ** -0.5)
        v = v_ref[:, vc].astype(BF16)
        scores = lax.dot_general(q.astype(BF16), k.astype(BF16), (((1,), (1,)), ((), ())),
                                 preferred_element_type=F32) * din_ref[h]
        s_old = s_ref[h]
        lhs = jnp.concatenate([scores.astype(BF16), (q * qd_ref[h]).astype(BF16)], axis=1)
        rhs = jnp.concatenate([v, s_old.astype(BF16)], axis=0)
        o = jnp.dot(lhs, rhs, preferred_element_type=F32)
        kt = jnp.transpose(k * kd_ref[h]).astype(BF16)
        s_ref[h] = cdec_ref[h] * s_old + jnp.dot(kt, v, preferred_element_type=F32)
        on = _group_norm(o)
        x_ref[:, vc] = (jax.nn.silu(g_ref[:, vc]) * on).astype(x_ref.dtype)


def _ret_sample_token(cdec_ref, row_ref, cos_ref, se_ref, so_ref, gq_ref, s_ref, x_ref, so_out_ref):
    row = row_ref[0]
    q = _rotate(row[:, 0:RET_HK], cos_ref[...], se_ref[...], so_ref[...])
    k = _rotate(row[:, RET_HK:2 * RET_HK], cos_ref[...], se_ref[...], so_ref[...]) * (RET_DK ** -0.5)
    v = row[:, 2 * RET_HK:2 * RET_HK + RET_HV]
    g = row[:, 2 * RET_HK + RET_HV:]
    qk = q * k

    pad = LANES
    head_of_row = lax.broadcasted_iota(jnp.int32, (pad, RET_HK), 0)
    head_of_col = lax.shift_right_logical(lax.broadcasted_iota(jnp.int32, (pad, RET_HK), 1),
                                          int(math.log2(RET_DK)))
    on_diag = head_of_row == head_of_col
    q_bd = jnp.where(on_diag[:SUBLANES], q * gq_ref[...], 0.0).astype(BF16)
    k_bd = jnp.where(on_diag, k, 0.0)
    s_old = s_ref[0]
    cross = jnp.dot(q_bd, s_old.astype(BF16), preferred_element_type=F32)
    v_row_id = lax.broadcasted_iota(jnp.int32, (pad, RET_DV), 0)
    v_rows = jnp.zeros((pad, RET_DV), F32)
    for h in range(RET_HEADS):
        v_rows = jnp.where(v_row_id == h, v[:, h * RET_DV:(h + 1) * RET_DV], v_rows)
    k_cols = jnp.transpose(k_bd).astype(BF16)
    v_rows = v_rows.astype(BF16)
    for h in range(RET_HEADS):
        kc = slice(h * RET_DK, (h + 1) * RET_DK)
        vc = slice(h * RET_DV, (h + 1) * RET_DV)
        score = jnp.sum(qk[:, kc], axis=-1, keepdims=True)
        on = _group_norm(score * v[:, vc] + cross[h:h + 1, :])
        x_ref[0, :, vc] = (jax.nn.silu(g[:, vc]) * on).astype(x_ref.dtype)
        outer = jnp.dot(k_cols[kc, :], v_rows, preferred_element_type=F32)
        so_out_ref[0, kc, :] = cdec_ref[h] * s_old[kc, :] + outer


N_SAMPLE_IN_REFS = 7


def _ret_prompt_kernel(*refs):
    s_ref = refs[-1]

    @pl.when(pl.program_id(1) == 0)
    def _():
        s_ref[...] = jnp.zeros_like(s_ref)

    _ret_prompt_heads(range(RET_HEADS), *refs[:-1], s_ref.at[0])


RET_SAMPLE_TOKENS_PER_STEP = 2


def _ret_sample_kernel(*refs):
    cdec_ref, row_ref, cos_ref, se_ref, so_ref, gq_ref, s_ref = refs[:N_SAMPLE_IN_REFS]
    x_ref, so_out_ref = refs[-2:]
    for t in range(RET_SAMPLE_TOKENS_PER_STEP):
        one = pl.ds(t, 1)
        _ret_sample_token(cdec_ref, row_ref.at[one], cos_ref, se_ref, so_ref, gq_ref, s_ref.at[one],
                          x_ref.at[one], so_out_ref.at[one])


def retention(qkvg_p, qkvg_s, batch, seq, state_all, li, new_states):
    c = RET_CHUNK
    n = seq // c
    bs, n_in = qkvg_s.shape
    n_layers = state_all.shape[0]
    lg = _log_gamma()
    idx = jnp.arange(c, dtype=F32)
    diff = idx[:, None] - idx[None, :]
    decay_in = jnp.where(diff[None] >= 0, jnp.exp(lg[:, None, None] * jnp.maximum(diff, 0.0)[None]), 0.0)
    q_dec = jnp.exp(lg[:, None] * (idx + 1.0)[None, :])
    k_dec = jnp.exp(lg[:, None] * (c - 1.0 - idx)[None, :])
    chunk_dec = jnp.exp(lg * c)
    qd = jnp.broadcast_to(q_dec[:, :, None], (RET_HEADS, c, RET_DK))
    kd = jnp.broadcast_to(k_dec[:, :, None], (RET_HEADS, c, RET_DK))
    cos_p, se_p, so_p = _rotary_tables(jnp.arange(seq, dtype=jnp.int32))
    gamma = jnp.exp(lg * 1.0)
    gq = jnp.repeat(gamma, RET_DK).reshape(1, RET_HK)
    cos_s, se_s, so_s = (jnp.tile(t, (1, RET_HEADS))
                         for t in _rotary_tables(PAST_LEN + jnp.arange(1, dtype=jnp.int32)))

    smem = pl.BlockSpec(memory_space=pltpu.SMEM)

    rows = lambda b, j: b * n + j
    tab = pl.BlockSpec((c, RET_DK), lambda b, j: (j, 0))
    full3 = lambda shape: pl.BlockSpec(shape, lambda b, j: (0, 0, 0))
    xp, sp = pl.pallas_call(
        _ret_prompt_kernel,
        out_shape=(jax.ShapeDtypeStruct((batch * seq, RET_HV), BF16),
                   jax.ShapeDtypeStruct((batch, RET_HEADS, RET_DK, RET_DV), F32)),
        grid=(batch, n),
        in_specs=[
            smem,
            pl.BlockSpec((c, RET_HK), lambda b, j: (rows(b, j), 0)),
            pl.BlockSpec((c, RET_HK), lambda b, j: (rows(b, j), 1)),
            pl.BlockSpec((c, RET_HV), lambda b, j: (rows(b, j), 1)),
            pl.BlockSpec((c, RET_HV), lambda b, j: (rows(b, j), 2)),
            tab, tab, tab,
            full3((RET_HEADS, c, c)), full3((RET_HEADS, c, RET_DK)), full3((RET_HEADS, c, RET_DK)),
        ],
        out_specs=(pl.BlockSpec((c, RET_HV), lambda b, j: (rows(b, j), 0)),
                   pl.BlockSpec((1, RET_HEADS, RET_DK, RET_DV), lambda b, j: (b, 0, 0, 0))),
        compiler_params=_params("parallel", "arbitrary"),
        name="retention_prompt",
    )(chunk_dec, qkvg_p, qkvg_p, qkvg_p, qkvg_p, cos_p, se_p, so_p, decay_in, qd, kd)

    tps = RET_SAMPLE_TOKENS_PER_STEP
    assert bs % tps == 0
    vec = pl.BlockSpec((1, RET_HK), lambda i: (0, 0))
    state_spec = pl.BlockSpec((None, tps, RET_HK, RET_DV), lambda i: (li, i, 0, 0))
    in_specs = [smem, pl.BlockSpec((tps, 1, n_in), lambda i: (i, 0, 0)), vec, vec, vec, vec, state_spec]
    args = [gamma, qkvg_s.reshape(bs, 1, n_in), cos_s, se_s, so_s, gq,
            state_all.reshape(n_layers, bs, RET_HK, RET_DV)]
    assert len(args) == N_SAMPLE_IN_REFS
    aliases = {}
    if new_states is not None:
        aliases = {len(args): 1}
        in_specs.append(pl.BlockSpec(memory_space=pl.ANY))
        args.append(new_states)
    xs, ss = pl.pallas_call(
        _ret_sample_kernel,
        out_shape=(jax.ShapeDtypeStruct((bs, 1, RET_HV), BF16),
                   jax.ShapeDtypeStruct((n_layers, bs, RET_HK, RET_DV), F32)),
        grid=(bs // tps,),
        in_specs=in_specs,
        out_specs=(pl.BlockSpec((tps, 1, RET_HV), lambda i: (i, 0, 0)), state_spec),
        input_output_aliases=aliases,
        compiler_params=_params("parallel"),
        name="retention_sample",
    )(*args)
    return xp, sp, xs.reshape(bs, RET_HV), ss


def _proj_res_kernel(xp_ref, xs_ref, w_ref, hp_ref, hs_ref, g_ref, hop_ref, hnp_ref, hos_ref, hns_ref, w_s,
                     *, k_rows, n_chunks, n_blocks):
    step = pl.program_id(0)
    chunk = w_ref.shape[0]

    @pl.when(step < n_chunks)
    def _():
        if w_s.shape[0] > k_rows:
            @pl.when(step == 0)
            def _():
                w_s[k_rows:, :] = jnp.zeros((w_s.shape[0] - k_rows, w_s.shape[1]), BF16)

        start = pl.multiple_of(step * chunk, 2 * SUBLANES)
        w_s[pl.ds(start, chunk), :] = w_ref[...].astype(BF16)

    def run(x_ref, h_ref, ho_ref, hn_ref):
        h_new = h_ref[...] + jnp.dot(x_ref[...], w_s[...], preferred_element_type=F32)
        ho_ref[...] = h_new
        hn_ref[...] = _rms(h_new, g_ref[...]).astype(hn_ref.dtype)

    @pl.when(jnp.logical_and(step >= n_chunks, step < n_chunks + n_blocks))
    def _():
        run(xp_ref, hp_ref, hop_ref, hnp_ref)

    @pl.when(step == n_chunks + n_blocks)
    def _():
        run(xs_ref, hs_ref, hos_ref, hns_ref)


def proj_residual(xp, xs, w_all, layer, hp, hs, g_next, chunk, name):
    tp, k = xp.shape
    ts = xs.shape[0]
    k_rows, d = w_all.shape[1], w_all.shape[2]
    n_chunks = k_rows // chunk
    assert chunk * n_chunks == k_rows and chunk % (2 * SUBLANES) == 0
    tm = PROJ_TM
    nb = tp // tm
    blk = lambda s: (jnp.clip(s - n_chunks, 0, nb - 1), 0)
    fixed = lambda s: (0, 0)
    f32 = lambda rows: jax.ShapeDtypeStruct((rows, d), F32)
    bf16 = lambda rows: jax.ShapeDtypeStruct((rows, d), BF16)
    return pl.pallas_call(
        functools.partial(_proj_res_kernel, k_rows=k_rows, n_chunks=n_chunks, n_blocks=nb),
        out_shape=(f32(tp), bf16(tp), f32(ts), bf16(ts)),
        grid=(n_chunks + nb + 1,),
        in_specs=[
            pl.BlockSpec((tm, k), blk),
            pl.BlockSpec((ts, k), fixed),
            pl.BlockSpec((None, chunk, d), lambda s: (layer, jnp.minimum(s, n_chunks - 1), 0)),
            pl.BlockSpec((tm, d), blk),
            pl.BlockSpec((ts, d), fixed),
            pl.BlockSpec((1, d), fixed),
        ],
        out_specs=(pl.BlockSpec((tm, d), blk), pl.BlockSpec((tm, d), blk),
                   pl.BlockSpec((ts, d), fixed), pl.BlockSpec((ts, d), fixed)),
        scratch_shapes=[pltpu.VMEM((k, d), BF16)],
        compiler_params=_params("arbitrary"),
        name=name,
    )(xp, xs, w_all, hp, hs, g_next.reshape(1, d))


def _conv_gate(a, a1, a2, b, cw_ref, cb_ref):
    conv = cb_ref[...] + cw_ref[0:1, :] * a2 + cw_ref[1:2, :] * a1 + cw_ref[2:3, :] * a
    return jax.nn.gelu(conv) * b


FFN_W_BLOCKS = FFN_TN // LANES
FFN_COL_BLOCKS = D_FF // LANES


def _ffn_weight_specs(layer, d, col_tile_of):
    last = 2 * FFN_COL_BLOCKS - 1

    def spec(first_block, r):
        def index(*grid_idx):
            return (layer, 0, jnp.minimum(first_block + FFN_W_BLOCKS * col_tile_of(*grid_idx) + r, last))
        return pl.BlockSpec((None, d, LANES), index)

    return ([spec(0, r) for r in range(FFN_W_BLOCKS)]
            + [spec(FFN_COL_BLOCKS, r) for r in range(FFN_W_BLOCKS)])


def _cast_weight_blocks(block_refs, dst):
    for r, ref in enumerate(block_refs):
        dst[:, r * LANES:(r + 1) * LANES] = ref[...].astype(BF16)


def _valid_cols(col_tile, shape):
    return col_tile * FFN_TN + lax.broadcasted_iota(jnp.int32, shape, 1) < D_FF


def _ffn_in_kernel(xp_ref, xs_ref, *refs):
    wa_refs, wb_refs = refs[:FFN_W_BLOCKS], refs[FFN_W_BLOCKS:2 * FFN_W_BLOCKS]
    cw_ref, cb_ref, b0_ref, b1_ref, yp_ref, st_ref, ys_ref, as_ref, wa_s, wb_s = refs[2 * FFN_W_BLOCKS:]
    col_tile = pl.program_id(0)
    b = pl.program_id(1)

    @pl.when(b == 0)
    def _():
        _cast_weight_blocks(wa_refs, wa_s)
        _cast_weight_blocks(wb_refs, wb_s)
        x = xs_ref[...]
        a = jnp.dot(x, wa_s[...], preferred_element_type=F32)
        g = jnp.dot(x, wb_s[...], preferred_element_type=F32)
        as_ref[...] = a
        y = jnp.where(_valid_cols(col_tile, a.shape),
                      _conv_gate(a, b1_ref[...], b0_ref[...], g, cw_ref, cb_ref), 0.0)
        ys_ref[...] = y.astype(ys_ref.dtype)

    @pl.when(b > 0)
    def _():
        seq = xp_ref.shape[0]
        rc = FFN_ROW_CHUNK
        tn = FFN_TN
        rows = lax.broadcasted_iota(jnp.int32, (rc, tn), 0)
        valid = _valid_cols(col_tile, (rc, tn))
        prev1 = jnp.zeros((1, tn), F32)
        prev2 = jnp.zeros((1, tn), F32)
        for r in range(seq // rc):
            x = xp_ref[r * rc:(r + 1) * rc, :]
            a = jnp.dot(x, wa_s[...], preferred_element_type=F32)
            g = jnp.dot(x, wb_s[...], preferred_element_type=F32)
            a1 = jnp.where(rows >= 1, pltpu.roll(a, 1, 0), prev1)
            a2 = jnp.where(rows >= 2, pltpu.roll(a, 2, 0), jnp.where(rows == 1, prev1, prev2))
            y = jnp.where(valid, _conv_gate(a, a1, a2, g, cw_ref, cb_ref), 0.0)
            yp_ref[r * rc:(r + 1) * rc, :] = y.astype(yp_ref.dtype)
            prev1 = a[rc - 1:rc, :]
            prev2 = a[rc - 2:rc - 1, :]
            if r == seq // rc - 1:
                st_ref[0] = a[rc - SUBLANES:, :]


def ffn_in(hnp, hns, w_in_all, layer, cw, cb, buf, batch, seq):
    tn = FFN_TN
    d = hnp.shape[1]
    ts = hns.shape[0]
    pad = ((0, 0), (0, D_FF_PAD - D_FF))
    b0 = jnp.pad(buf[:, 0, :], pad)
    b1 = jnp.pad(buf[:, 1, :], pad)
    sq = lambda b: jnp.maximum(b - 1, 0)
    col = lambda rows_: pl.BlockSpec((rows_, tn), lambda j, b: (0, j))
    yp, st, ys, a_s = pl.pallas_call(
        _ffn_in_kernel,
        out_shape=(jax.ShapeDtypeStruct((batch * seq, D_FF_PAD), BF16),
                   jax.ShapeDtypeStruct((batch, SUBLANES, D_FF_PAD), F32),
                   jax.ShapeDtypeStruct((ts, D_FF_PAD), BF16),
                   jax.ShapeDtypeStruct((ts, D_FF_PAD), F32)),
        grid=(D_FF_PAD // tn, batch + 1),
        in_specs=[pl.BlockSpec((seq, d), lambda j, b: (sq(b), 0)), pl.BlockSpec((ts, d), lambda j, b: (0, 0))]
        + _ffn_weight_specs(layer, d, lambda j, b: j)
        + [col(CONV_W), col(1), col(ts), col(ts)],
        out_specs=(pl.BlockSpec((seq, tn), lambda j, b: (sq(b), j)),
                   pl.BlockSpec((1, SUBLANES, tn), lambda j, b: (sq(b), 0, j)),
                   col(ts), col(ts)),
        scratch_shapes=[pltpu.VMEM((d, tn), BF16), pltpu.VMEM((d, tn), BF16)],
        compiler_params=_params("arbitrary", "arbitrary"),
        name="ffn_in",
    )(hnp, hns, *([w_in_all] * (2 * FFN_W_BLOCKS)), cw, cb, b0, b1)
    conv_p = st[:, SUBLANES - (CONV_W - 1):, :D_FF]
    conv_s = jnp.stack([buf[:, 1, :], a_s[:, :D_FF]], axis=1)
    return yp, ys, conv_p, conv_s


def _ple_kernel(hnp_ref, hns_ref, wg_ref, pp_ref, ps_ref, wp_ref, hp_ref, hs_ref, g_ref, *refs, emit_h, n_blocks):
    out_refs, (wg_s, wp_s) = refs[:-2], refs[-2:]
    n_out = len(out_refs) // 2
    i = pl.program_id(0)

    @pl.when(i == 0)
    def _():
        wg_s[...] = wg_ref[...].astype(BF16)
        wp_s[...] = wp_ref[...].astype(BF16)

    def run(hn_ref, p_ref, h_ref, outs):
        gate = jax.nn.sigmoid(jnp.dot(hn_ref[...], wg_s[...], preferred_element_type=F32))
        proj = jnp.dot(p_ref[...].astype(BF16), wp_s[...], preferred_element_type=F32)
        h_new = h_ref[...] + gate * proj
        if emit_h:
            outs[0][...] = h_new
        outs[-1][...] = _rms(h_new, g_ref[...]).astype(outs[-1].dtype)

    @pl.when(i < n_blocks)
    def _():
        run(hnp_ref, pp_ref, hp_ref, out_refs[:n_out])

    @pl.when(i == n_blocks)
    def _():
        run(hns_ref, ps_ref, hs_ref, out_refs[n_out:])


def ple_residual(hnp, hns, wg_all, pp_all, ps_all, layer, wp_all, hp, hs, g_next, emit_h):
    tp, d = hp.shape
    ts = hs.shape[0]
    pd = pp_all.shape[-1]
    pp_all = pp_all.reshape(pp_all.shape[0], tp, pd)
    ps_all = ps_all.reshape(ps_all.shape[0], ts, pd)
    tm = PLE_TM
    nb = tp // tm
    blk = lambda i: jnp.minimum(i, nb - 1)
    prow = lambda width: pl.BlockSpec((tm, width), lambda i: (blk(i), 0))
    srow = lambda width: pl.BlockSpec((ts, width), lambda i: (0, 0))
    const = lambda r, c: pl.BlockSpec((None, r, c), lambda i: (layer, 0, 0), pipeline_mode=pl.Buffered(1))
    if emit_h:
        out_shape = tuple(jax.ShapeDtypeStruct((rows, d), dt) for rows in (tp, ts) for dt in (F32, BF16))
        out_specs = (prow(d), prow(d), srow(d), srow(d))
    else:
        out_shape = (jax.ShapeDtypeStruct((tp, d), F32), jax.ShapeDtypeStruct((ts, d), F32))
        out_specs = (prow(d), srow(d))
    return pl.pallas_call(
        functools.partial(_ple_kernel, emit_h=emit_h, n_blocks=nb),
        out_shape=out_shape,
        grid=(nb + 1,),
        in_specs=[prow(d), srow(d), const(d, d),
                  pl.BlockSpec((None, tm, pd), lambda i: (layer, blk(i), 0)),
                  pl.BlockSpec((None, ts, pd), lambda i: (layer, 0, 0)),
                  const(pd, d), prow(d), srow(d), pl.BlockSpec((1, d), lambda i: (0, 0))],
        out_specs=out_specs,
        scratch_shapes=[pltpu.VMEM((d, d), BF16), pltpu.VMEM((pd, d), BF16)],
        compiler_params=_params("arbitrary"),
        name="ple_residual",
    )(hnp, hns, wg_all, pp_all, ps_all, wp_all, hp, hs, g_next.reshape(1, d))


def _trunks(x_prompt, x_sample, state_ret, state_conv, p_prompt, p_sample, w):
    bp, lp = x_prompt.shape[0], x_prompt.shape[1]
    bs, ls = x_sample.shape[0], x_sample.shape[1]
    hp = x_prompt.reshape(bp * lp, D_MODEL)
    hs = x_sample.reshape(bs * ls, D_MODEL)
    hnp = rmsnorm_rows(hp, w["norm_mix_g"][0], BF16)
    hns = rmsnorm_rows(hs, w["norm_mix_g"][0], BF16)
    ret_p, conv_p, conv_s, v_s = [], [], [], []
    ret_s = None
    for i in range(DEPTH):
        li = i // N_MIXERS
        if i % N_MIXERS == 0:
            zp, zs = matmul_act(hnp, hns, w["gmlp_w_in"], li, "gelu", BF16, "gmlp_in")
            gate_w = (w["gmlp_ln_g"][li], w["gmlp_ln_b"][li], w["gmlp_w_s"][li], w["gmlp_b_s"][li])
            xmp = gmlp_gate_prompt(zp, *gate_w)
            xms, vn = gmlp_gate_sample(zs, *gate_w)
            v_s.append(vn.reshape(bs, ls, GMLP_WIDTH))
            w_out = w["gmlp_w_out"]
        else:
            qp, qs = matmul_act(hnp, hns, w["ret_w_in"], li, None, F32, "ret_in")
            xmp, s_new, xms, ret_s = retention(qp, qs, bp, lp, state_ret, li, ret_s)
            ret_p.append(s_new)
            w_out = w["ret_w_out"]
        hp, hnp, hs, hns = proj_residual(xmp, xms, w_out, li, hp, hs, w["norm_ffn_g"][i],
                                         MIXER_OUT_W_CHUNK, "mixer_out")
        yp, ys, cp, cs = ffn_in(hnp, hns, w["ffn_w_in"], i, w["ffn_cw"][i], w["ffn_cb"][i], state_conv[i], bp, lp)
        conv_p.append(cp)
        conv_s.append(cs)
        hp, hnp, hs, hns = proj_residual(yp, ys, w["ffn_w_down"], i, hp, hs, w["norm_ple_g"][i],
                                         FFN_DOWN_W_CHUNK, "ffn_down")
        last = i == DEPTH - 1
        g_next = w["final_norm_g"] if last else w["norm_mix_g"][i + 1]
        outs = ple_residual(hnp, hns, w["ple_w_gate"], p_prompt, p_sample, i, w["ple_w_proj"], hp, hs, g_next,
                            not last)
        if last:
            y_p, y_s = outs
        else:
            hp, hnp, hs, hns = outs
    return (y_p.reshape(bp, lp, D_MODEL), y_s.reshape(bs, ls, D_MODEL),
            jnp.stack(ret_p), ret_s.reshape(state_ret.shape),
            jnp.stack(conv_p), jnp.stack(conv_s), jnp.stack(v_s))


def kernel(x_prompt, x_sample, state_ret, state_conv, p_prompt, p_sample, norm_mix_g, norm_ffn_g, norm_ple_g, final_norm_g, gmlp_w_in, gmlp_ln_g, gmlp_ln_b, gmlp_w_s, gmlp_b_s, gmlp_w_out, ret_w_in, ret_w_out, ffn_w_in, ffn_conv_w, ffn_conv_b, ffn_w_down, ple_w_proj, ple_w_gate):
    ff_pad = D_FF_PAD - D_FF
    w = {
        "norm_mix_g": norm_mix_g, "norm_ffn_g": norm_ffn_g, "norm_ple_g": norm_ple_g, "final_norm_g": final_norm_g,
        "gmlp_w_in": gmlp_w_in, "gmlp_ln_g": gmlp_ln_g, "gmlp_ln_b": gmlp_ln_b,
        "gmlp_w_s": gmlp_w_s, "gmlp_b_s": gmlp_b_s, "gmlp_w_out": gmlp_w_out,
        "ret_w_in": ret_w_in, "ret_w_out": ret_w_out,
        "ffn_w_in": ffn_w_in,
        "ffn_cw": jnp.pad(ffn_conv_w, ((0, 0), (0, 0), (0, ff_pad))),
        "ffn_cb": jnp.pad(ffn_conv_b, ((0, 0), (0, ff_pad))).reshape(DEPTH, 1, D_FF_PAD),
        "ffn_w_down": ffn_w_down,
        "ple_w_proj": ple_w_proj, "ple_w_gate": ple_w_gate,
    }
    return _trunks(x_prompt, x_sample, state_ret, state_conv, p_prompt, p_sample, w)
```

```python
import functools
import math

import jax
import jax.numpy as jnp
from jax import lax
from jax.experimental import pallas as pl
from jax.experimental.pallas import tpu as pltpu

F32 = jnp.float32
BF16 = jnp.bfloat16

D_MODEL = 2048
DEPTH = 4
PAST_LEN = 16384
N_MIXERS = 2
PLE_DIM = 256
GMLP_CHUNK = 128
GMLP_WIDTH = 2 * D_MODEL
GMLP_GROUPS = 8
GMLP_GROUP_DIM = GMLP_WIDTH // GMLP_GROUPS
RET_HEADS = 8
RET_DK = D_MODEL // RET_HEADS
RET_DV = 2 * RET_DK
RET_HK = RET_HEADS * RET_DK
RET_HV = RET_HEADS * RET_DV
RET_CHUNK = 128
ROT_BASE = 10000.0
D_FF = ((8 * D_MODEL // 3 + 127) // 128) * 128
CONV_W = 3
EPS = 1e-6

V7X_VMEM_BYTES = 64 * 1024 * 1024
VMEM_LIMIT_BYTES = V7X_VMEM_BYTES - 8 * 1024 * 1024
SUBLANES = 8
LANES = 128
MXU_DEPTH = 256
D_FF_PAD = -(-D_FF // (2 * MXU_DEPTH)) * (2 * MXU_DEPTH)
FFN_TN = 512
FFN_ROW_CHUNK = 1024
MM_TM = 1024
MM_TN = 1024
PROJ_TM = 256
PLE_TM = 256
MIXER_OUT_W_CHUNK = 512
FFN_DOWN_W_CHUNK = 128


def _params(*sem):
    return pltpu.CompilerParams(dimension_semantics=sem, vmem_limit_bytes=VMEM_LIMIT_BYTES)


def _rms(x, g):
    r = lax.rsqrt(jnp.mean(x * x, axis=-1, keepdims=True) + EPS)
    return x * r * g


def _group_norm(o):
    mu = jnp.mean(o, axis=-1, keepdims=True)
    d = o - mu
    var = jnp.mean(d * d, axis=-1, keepdims=True)
    return d * lax.rsqrt(var + EPS)


def _rotate(x, cos, sin_even, sin_odd):
    n = x.shape[-1]
    return x * cos + pltpu.roll(x, n - 1, 1) * sin_even + pltpu.roll(x, 1, 1) * sin_odd


def _rms_kernel(x_ref, g_ref, o_ref):
    o_ref[...] = _rms(x_ref[...], g_ref[...]).astype(o_ref.dtype)


def rmsnorm_rows(x, g, out_dtype):
    t, d = x.shape
    tm = min(256, t)
    return pl.pallas_call(
        _rms_kernel,
        out_shape=jax.ShapeDtypeStruct((t, d), out_dtype),
        grid=(t // tm,),
        in_specs=[pl.BlockSpec((tm, d), lambda i: (i, 0)), pl.BlockSpec((1, d), lambda i: (0, 0))],
        out_specs=pl.BlockSpec((tm, d), lambda i: (i, 0)),
        compiler_params=_params("parallel"),
        name="rmsnorm_rows",
    )(x, g.reshape(1, d))


def _mm_kernel(xp_ref, xs_ref, w_ref, *refs, act, rotate):
    op_ref, os_ref, w_s = refs[-3:]
    tabs_p, tabs_s = refs[:3], refs[3:6]
    i = pl.program_id(1)

    def run(x_ref, o_ref, tabs):
        acc = jnp.dot(x_ref[...], w_s[...], preferred_element_type=F32)
        if act == "gelu":
            acc = jax.nn.gelu(acc)
        if rotate:
            cos, se, so = (t[...] for t in tabs)
            for h in range(acc.shape[1] // RET_DK):
                cols = slice(h * RET_DK, (h + 1) * RET_DK)
                o_ref[:, cols] = _rotate(acc[:, cols], cos, se, so).astype(o_ref.dtype)
        else:
            o_ref[...] = acc.astype(o_ref.dtype)

    @pl.when(i == 0)
    def _():
        w_s[...] = w_ref[...].astype(BF16)
        run(xs_ref, os_ref, tabs_s)

    @pl.when(i > 0)
    def _():
        run(xp_ref, op_ref, tabs_p)


def matmul_act(xp, xs, w_all, layer, act, prompt_dtype, name, cols=None, rotary=None):
    tp, k = xp.shape
    ts = xs.shape[0]
    col0, n = cols if cols is not None else (0, w_all.shape[2])
    tm = MM_TM
    tn = MM_TN
    assert col0 % tn == 0 and n % tn == 0
    tile0 = col0 // tn
    nb = tp // tm
    blk = lambda i: jnp.maximum(i - 1, 0)
    in_specs = [pl.BlockSpec((tm, k), lambda j, i: (blk(i), 0)),
                pl.BlockSpec((ts, k), lambda j, i: (0, 0)),
                pl.BlockSpec((None, k, tn), lambda j, i: (layer, 0, tile0 + j))]
    args = [xp, xs, w_all]
    if rotary is not None:
        tabs_p, tabs_s = rotary
        seq = tabs_p[0].shape[0]
        assert seq % tm == 0
        blocks_per_seq = seq // tm
        in_specs += [pl.BlockSpec((tm, RET_DK), lambda j, i: (blk(i) % blocks_per_seq, 0))] * 3
        in_specs += [pl.BlockSpec((1, RET_DK), lambda j, i: (0, 0))] * 3
        args += [*tabs_p, *tabs_s]
    return pl.pallas_call(
        functools.partial(_mm_kernel, act=act, rotate=rotary is not None),
        out_shape=(jax.ShapeDtypeStruct((tp, n), prompt_dtype), jax.ShapeDtypeStruct((ts, n), F32)),
        grid=(n // tn, nb + 1),
        in_specs=in_specs,
        out_specs=(pl.BlockSpec((tm, tn), lambda j, i: (blk(i), j)),
                   pl.BlockSpec((ts, tn), lambda j, i: (0, j))),
        scratch_shapes=[pltpu.VMEM((k, tn), BF16)],
        compiler_params=_params("arbitrary", "arbitrary"),
        name=name,
    )(*args)


def _layernorm(v, g, b):
    mu = jnp.mean(v, axis=-1, keepdims=True)
    d = v - mu
    var = jnp.mean(d * d, axis=-1, keepdims=True)
    return d * lax.rsqrt(var + EPS) * g + b


GATE_CHUNKS_PER_STEP = 2


def _gmlp_gate_kernel(u_ref, v_ref, lng_ref, lnb_ref, ws_ref, bs_ref, x_ref):
    c = GMLP_CHUNK
    causal = lax.broadcasted_iota(jnp.int32, (c, c), 0) >= lax.broadcasted_iota(jnp.int32, (c, c), 1)
    for r in range(GATE_CHUNKS_PER_STEP):
        rows = slice(r * c, (r + 1) * c)
        vn = _layernorm(v_ref[rows, :].astype(F32), lng_ref[...], lnb_ref[...]).astype(BF16)
        for g in range(GMLP_GROUPS):
            cols = slice(g * GMLP_GROUP_DIM, (g + 1) * GMLP_GROUP_DIM)
            w = jnp.where(causal, ws_ref[g], 0.0).astype(BF16)
            s = jnp.dot(w, vn[:, cols], preferred_element_type=F32) + bs_ref[g]
            x_ref[rows, cols] = (u_ref[rows, cols].astype(F32) * s).astype(x_ref.dtype)


def gmlp_gate_prompt(z, ln_g, ln_b, w_s, b_s):
    t = z.shape[0]
    w = GMLP_WIDTH
    c = GMLP_CHUNK
    tm = c * GATE_CHUNKS_PER_STEP
    bs_b = jnp.broadcast_to(b_s[:, :, None], (GMLP_GROUPS, c, GMLP_GROUP_DIM))
    return pl.pallas_call(
        _gmlp_gate_kernel,
        out_shape=jax.ShapeDtypeStruct((t, w), BF16),
        grid=(t // tm,),
        in_specs=[
            pl.BlockSpec((tm, w), lambda i: (i, 0)),
            pl.BlockSpec((tm, w), lambda i: (i, 1)),
            pl.BlockSpec((1, w), lambda i: (0, 0)),
            pl.BlockSpec((1, w), lambda i: (0, 0)),
            pl.BlockSpec((GMLP_GROUPS, c, c), lambda i: (0, 0, 0)),
            pl.BlockSpec((GMLP_GROUPS, c, GMLP_GROUP_DIM), lambda i: (0, 0, 0)),
        ],
        out_specs=pl.BlockSpec((tm, w), lambda i: (i, 0)),
        compiler_params=_params("parallel"),
        name="gmlp_gate_prompt",
    )(z, z, ln_g.reshape(1, w), ln_b.reshape(1, w), w_s, bs_b)


def _gmlp_gate_sample_kernel(u_ref, v_ref, lng_ref, lnb_ref, scale_ref, shift_ref, x_ref, vn_ref):
    vn = _layernorm(v_ref[...], lng_ref[...], lnb_ref[...])
    vn_ref[...] = vn
    x_ref[...] = (u_ref[...] * (vn * scale_ref[...] + shift_ref[...])).astype(x_ref.dtype)


def gmlp_gate_sample(z, ln_g, ln_b, w_s, b_s):
    t = z.shape[0]
    w = GMLP_WIDTH
    scale = jnp.repeat(w_s[:, 0, 0], GMLP_GROUP_DIM).reshape(1, w)
    shift = jnp.repeat(b_s[:, 0], GMLP_GROUP_DIM).reshape(1, w)
    row = pl.BlockSpec((1, w), lambda i: (0, 0))
    return pl.pallas_call(
        _gmlp_gate_sample_kernel,
        out_shape=(jax.ShapeDtypeStruct((t, w), BF16), jax.ShapeDtypeStruct((t, w), F32)),
        grid=(1,),
        in_specs=[pl.BlockSpec((t, w), lambda i: (0, 0)), pl.BlockSpec((t, w), lambda i: (0, 1)), row, row, row, row],
        out_specs=(pl.BlockSpec((t, w), lambda i: (0, 0)), pl.BlockSpec((t, w), lambda i: (0, 0))),
        compiler_params=_params("arbitrary"),
        name="gmlp_gate_sample",
    )(z, z, ln_g.reshape(1, w), ln_b.reshape(1, w), scale, shift)


def _rotary_tables(pos):
    angle = 1.0 / (ROT_BASE ** jnp.linspace(0.0, 1.0, RET_DK // 2, dtype=F32))
    ph = pos.astype(F32)[:, None] * angle[None, :]
    cos = jnp.cos(ph)
    sin = jnp.sin(ph)
    zero = jnp.zeros_like(sin)
    n = pos.shape[0]
    cos_r = jnp.stack([cos, cos], axis=-1).reshape(n, RET_DK)
    sin_even = jnp.stack([-sin, zero], axis=-1).reshape(n, RET_DK)
    sin_odd = jnp.stack([zero, sin], axis=-1).reshape(n, RET_DK)
    return cos_r, sin_even, sin_odd


def _log_gamma():
    return jnp.log(1.0 - 2.0 ** (-5.0 - jnp.arange(RET_HEADS, dtype=F32)))


def _ret_prompt_heads(heads, cdec_ref, q_ref, k_ref, v_ref, g_ref, cos_ref, se_ref, so_ref,
                      din_ref, qd_ref, kd_ref, x_ref, s_ref):
    cos = cos_ref[...]
    se = se_ref[...]
    so = so_ref[...]
    for h in heads:
        kc = slice(h * RET_DK, (h + 1) * RET_DK)
        vc = slice(h * RET_DV, (h + 1) * RET_DV)
        q = q_ref[:, kc]
        k = k_ref[:, kc] * (RET_DK ** -0.5)
        v = v_ref[:, vc].astype(BF16)
        scores = lax.dot_general(q.astype(BF16), k.astype(BF16), (((1,), (1,)), ((), ())),
                                 preferred_element_type=F32) * din_ref[h]
        s_old = s_ref[h]
        lhs = jnp.concatenate([scores.astype(BF16), (q * qd_ref[h]).astype(BF16)], axis=1)
        rhs = jnp.concatenate([v, s_old.astype(BF16)], axis=0)
        o = jnp.dot(lhs, rhs, preferred_element_type=F32)
        kt = jnp.transpose(k * kd_ref[h]).astype(BF16)
        s_ref[h] = cdec_ref[h] * s_old + jnp.dot(kt, v, preferred_element_type=F32)
        on = _group_norm(o)
        x_ref[:, vc] = (jax.nn.silu(g_ref[:, vc]) * on).astype(x_ref.dtype)


def _ret_sample_token(cdec_ref, row_ref, cos_ref, se_ref, so_ref, gq_ref, s_ref, x_ref, so_out_ref):
    row = row_ref[0]
    q = row[:, 0:RET_HK]
    k = row[:, RET_HK:2 * RET_HK] * (RET_DK ** -0.5)
    v = row[:, 2 * RET_HK:2 * RET_HK + RET_HV]
    g = row[:, 2 * RET_HK + RET_HV:]
    qk = q * k

    pad = LANES
    head_of_row = lax.broadcasted_iota(jnp.int32, (pad, RET_HK), 0)
    head_of_col = lax.shift_right_logical(lax.broadcasted_iota(jnp.int32, (pad, RET_HK), 1),
                                          int(math.log2(RET_DK)))
    on_diag = head_of_row == head_of_col
    q_bd = jnp.where(on_diag[:SUBLANES], q * gq_ref[...], 0.0).astype(BF16)
    k_bd = jnp.where(on_diag, k, 0.0)
    s_old = s_ref[0]
    cross = jnp.dot(q_bd, s_old.astype(BF16), preferred_element_type=F32)
    v_row_id = lax.broadcasted_iota(jnp.int32, (pad, RET_DV), 0)
    v_rows = jnp.zeros((pad, RET_DV), F32)
    for h in range(RET_HEADS):
        v_rows = jnp.where(v_row_id == h, v[:, h * RET_DV:(h + 1) * RET_DV], v_rows)
    k_cols = jnp.transpose(k_bd).astype(BF16)
    v_rows = v_rows.astype(BF16)
    for h in range(RET_HEADS):
        kc = slice(h * RET_DK, (h + 1) * RET_DK)
        vc = slice(h * RET_DV, (h + 1) * RET_DV)
        score = jnp.sum(qk[:, kc], axis=-1, keepdims=True)
        on = _group_norm(score * v[:, vc] + cross[h:h + 1, :])
        x_ref[0, :, vc] = (jax.nn.silu(g[:, vc]) * on).astype(x_ref.dtype)
        outer = jnp.dot(k_cols[kc, :], v_rows, preferred_element_type=F32)
        so_out_ref[0, kc, :] = cdec_ref[h] * s_old[kc, :] + outer


N_SAMPLE_IN_REFS = 7


def _ret_prompt_kernel(*refs):
    s_ref = refs[-1]

    @pl.when(pl.program_id(1) == 0)
    def _():
        s_ref[...] = jnp.zeros_like(s_ref)

    _ret_prompt_heads(range(RET_HEADS), *refs[:-1], s_ref.at[0])


RET_SAMPLE_TOKENS_PER_STEP = 2


def _ret_sample_kernel(*refs):
    cdec_ref, row_ref, cos_ref, se_ref, so_ref, gq_ref, s_ref = refs[:N_SAMPLE_IN_REFS]
    x_ref, so_out_ref = refs[-2:]
    for t in range(RET_SAMPLE_TOKENS_PER_STEP):
        one = pl.ds(t, 1)
        _ret_sample_token(cdec_ref, row_ref.at[one], cos_ref, se_ref, so_ref, gq_ref, s_ref.at[one],
                          x_ref.at[one], so_out_ref.at[one])


def retention(qk_p, vg_p, qkvg_s, batch, seq, state_all, li, new_states):
    c = RET_CHUNK
    n = seq // c
    bs, n_in = qkvg_s.shape
    n_layers = state_all.shape[0]
    lg = _log_gamma()
    idx = jnp.arange(c, dtype=F32)
    diff = idx[:, None] - idx[None, :]
    decay_in = jnp.where(diff[None] >= 0, jnp.exp(lg[:, None, None] * jnp.maximum(diff, 0.0)[None]), 0.0)
    q_dec = jnp.exp(lg[:, None] * (idx + 1.0)[None, :])
    k_dec = jnp.exp(lg[:, None] * (c - 1.0 - idx)[None, :])
    chunk_dec = jnp.exp(lg * c)
    qd = jnp.broadcast_to(q_dec[:, :, None], (RET_HEADS, c, RET_DK))
    kd = jnp.broadcast_to(k_dec[:, :, None], (RET_HEADS, c, RET_DK))
    cos_p, se_p, so_p = _rotary_tables(jnp.arange(seq, dtype=jnp.int32))
    gamma = jnp.exp(lg * 1.0)
    gq = jnp.repeat(gamma, RET_DK).reshape(1, RET_HK)
    cos_s, se_s, so_s = (jnp.tile(t, (1, RET_HEADS))
                         for t in _rotary_tables(PAST_LEN + jnp.arange(1, dtype=jnp.int32)))

    smem = pl.BlockSpec(memory_space=pltpu.SMEM)

    rows = lambda b, j: b * n + j
    tab = pl.BlockSpec((c, RET_DK), lambda b, j: (j, 0))
    full3 = lambda shape: pl.BlockSpec(shape, lambda b, j: (0, 0, 0))
    xp, sp = pl.pallas_call(
        _ret_prompt_kernel,
        out_shape=(jax.ShapeDtypeStruct((batch * seq, RET_HV), BF16),
                   jax.ShapeDtypeStruct((batch, RET_HEADS, RET_DK, RET_DV), F32)),
        grid=(batch, n),
        in_specs=[
            smem,
            pl.BlockSpec((c, RET_HK), lambda b, j: (rows(b, j), 0)),
            pl.BlockSpec((c, RET_HK), lambda b, j: (rows(b, j), 1)),
            pl.BlockSpec((c, RET_HV), lambda b, j: (rows(b, j), 0)),
            pl.BlockSpec((c, RET_HV), lambda b, j: (rows(b, j), 1)),
            tab, tab, tab,
            full3((RET_HEADS, c, c)), full3((RET_HEADS, c, RET_DK)), full3((RET_HEADS, c, RET_DK)),
        ],
        out_specs=(pl.BlockSpec((c, RET_HV), lambda b, j: (rows(b, j), 0)),
                   pl.BlockSpec((1, RET_HEADS, RET_DK, RET_DV), lambda b, j: (b, 0, 0, 0))),
        compiler_params=_params("parallel", "arbitrary"),
        name="retention_prompt",
    )(chunk_dec, qk_p, qk_p, vg_p, vg_p, cos_p, se_p, so_p, decay_in, qd, kd)

    tps = RET_SAMPLE_TOKENS_PER_STEP
    assert bs % tps == 0
    vec = pl.BlockSpec((1, RET_HK), lambda i: (0, 0))
    state_spec = pl.BlockSpec((None, tps, RET_HK, RET_DV), lambda i: (li, i, 0, 0))
    in_specs = [smem, pl.BlockSpec((tps, 1, n_in), lambda i: (i, 0, 0)), vec, vec, vec, vec, state_spec]
    args = [gamma, qkvg_s.reshape(bs, 1, n_in), cos_s, se_s, so_s, gq,
            state_all.reshape(n_layers, bs, RET_HK, RET_DV)]
    assert len(args) == N_SAMPLE_IN_REFS
    aliases = {}
    if new_states is not None:
        aliases = {len(args): 1}
        in_specs.append(pl.BlockSpec(memory_space=pl.ANY))
        args.append(new_states)
    xs, ss = pl.pallas_call(
        _ret_sample_kernel,
        out_shape=(jax.ShapeDtypeStruct((bs, 1, RET_HV), BF16),
                   jax.ShapeDtypeStruct((n_layers, bs, RET_HK, RET_DV), F32)),
        grid=(bs // tps,),
        in_specs=in_specs,
        out_specs=(pl.BlockSpec((tps, 1, RET_HV), lambda i: (i, 0, 0)), state_spec),
        input_output_aliases=aliases,
        compiler_params=_params("parallel"),
        name="retention_sample",
    )(*args)
    return xp, sp, xs.reshape(bs, RET_HV), ss


def _proj_res_kernel(xp_ref, xs_ref, w_ref, hp_ref, hs_ref, g_ref, hop_ref, hnp_ref, hos_ref, hns_ref, w_s,
                     *, k_rows, n_chunks, n_blocks):
    step = pl.program_id(0)
    chunk = w_ref.shape[0]

    @pl.when(step < n_chunks)
    def _():
        if w_s.shape[0] > k_rows:
            @pl.when(step == 0)
            def _():
                w_s[k_rows:, :] = jnp.zeros((w_s.shape[0] - k_rows, w_s.shape[1]), BF16)

        start = pl.multiple_of(step * chunk, 2 * SUBLANES)
        w_s[pl.ds(start, chunk), :] = w_ref[...].astype(BF16)

    def run(x_ref, h_ref, ho_ref, hn_ref):
        h_new = h_ref[...] + jnp.dot(x_ref[...], w_s[...], preferred_element_type=F32)
        ho_ref[...] = h_new
        hn_ref[...] = _rms(h_new, g_ref[...]).astype(hn_ref.dtype)

    @pl.when(jnp.logical_and(step >= n_chunks, step < n_chunks + n_blocks))
    def _():
        run(xp_ref, hp_ref, hop_ref, hnp_ref)

    @pl.when(step == n_chunks + n_blocks)
    def _():
        run(xs_ref, hs_ref, hos_ref, hns_ref)


def proj_residual(xp, xs, w_all, layer, hp, hs, g_next, chunk, name):
    tp, k = xp.shape
    ts = xs.shape[0]
    k_rows, d = w_all.shape[1], w_all.shape[2]
    n_chunks = k_rows // chunk
    assert chunk * n_chunks == k_rows and chunk % (2 * SUBLANES) == 0
    tm = PROJ_TM
    nb = tp // tm
    blk = lambda s: (jnp.clip(s - n_chunks, 0, nb - 1), 0)
    fixed = lambda s: (0, 0)
    f32 = lambda rows: jax.ShapeDtypeStruct((rows, d), F32)
    bf16 = lambda rows: jax.ShapeDtypeStruct((rows, d), BF16)
    return pl.pallas_call(
        functools.partial(_proj_res_kernel, k_rows=k_rows, n_chunks=n_chunks, n_blocks=nb),
        out_shape=(f32(tp), bf16(tp), f32(ts), bf16(ts)),
        grid=(n_chunks + nb + 1,),
        in_specs=[
            pl.BlockSpec((tm, k), blk),
            pl.BlockSpec((ts, k), fixed),
            pl.BlockSpec((None, chunk, d), lambda s: (layer, jnp.minimum(s, n_chunks - 1), 0)),
            pl.BlockSpec((tm, d), blk),
            pl.BlockSpec((ts, d), fixed),
            pl.BlockSpec((1, d), fixed),
        ],
        out_specs=(pl.BlockSpec((tm, d), blk), pl.BlockSpec((tm, d), blk),
                   pl.BlockSpec((ts, d), fixed), pl.BlockSpec((ts, d), fixed)),
        scratch_shapes=[pltpu.VMEM((k, d), BF16)],
        compiler_params=_params("arbitrary"),
        name=name,
    )(xp, xs, w_all, hp, hs, g_next.reshape(1, d))


def _conv_gate(a, a1, a2, b, cw_ref, cb_ref):
    conv = cb_ref[...] + cw_ref[0:1, :] * a2 + cw_ref[1:2, :] * a1 + cw_ref[2:3, :] * a
    return jax.nn.gelu(conv) * b


FFN_W_BLOCKS = FFN_TN // LANES
FFN_COL_BLOCKS = D_FF // LANES


def _ffn_weight_specs(layer, d, col_tile_of):
    last = 2 * FFN_COL_BLOCKS - 1

    def spec(first_block, r):
        def index(*grid_idx):
            return (layer, 0, jnp.minimum(first_block + FFN_W_BLOCKS * col_tile_of(*grid_idx) + r, last))
        return pl.BlockSpec((None, d, LANES), index)

    return ([spec(0, r) for r in range(FFN_W_BLOCKS)]
            + [spec(FFN_COL_BLOCKS, r) for r in range(FFN_W_BLOCKS)])


def _cast_weight_blocks(block_refs, dst):
    for r, ref in enumerate(block_refs):
        dst[:, r * LANES:(r + 1) * LANES] = ref[...].astype(BF16)


def _valid_cols(col_tile, shape):
    return col_tile * FFN_TN + lax.broadcasted_iota(jnp.int32, shape, 1) < D_FF


def _ffn_in_kernel(xp_ref, xs_ref, *refs):
    wa_refs, wb_refs = refs[:FFN_W_BLOCKS], refs[FFN_W_BLOCKS:2 * FFN_W_BLOCKS]
    cw_ref, cb_ref, b0_ref, b1_ref, yp_ref, st_ref, ys_ref, as_ref, wa_s, wb_s = refs[2 * FFN_W_BLOCKS:]
    col_tile = pl.program_id(0)
    b = pl.program_id(1)

    @pl.when(b == 0)
    def _():
        _cast_weight_blocks(wa_refs, wa_s)
        _cast_weight_blocks(wb_refs, wb_s)
        x = xs_ref[...]
        a = jnp.dot(x, wa_s[...], preferred_element_type=F32)
        g = jnp.dot(x, wb_s[...], preferred_element_type=F32)
        as_ref[...] = a
        y = jnp.where(_valid_cols(col_tile, a.shape),
                      _conv_gate(a, b1_ref[...], b0_ref[...], g, cw_ref, cb_ref), 0.0)
        ys_ref[...] = y.astype(ys_ref.dtype)

    @pl.when(b > 0)
    def _():
        seq = xp_ref.shape[0]
        rc = FFN_ROW_CHUNK
        tn = FFN_TN
        rows = lax.broadcasted_iota(jnp.int32, (rc, tn), 0)
        valid = _valid_cols(col_tile, (rc, tn))
        prev1 = jnp.zeros((1, tn), F32)
        prev2 = jnp.zeros((1, tn), F32)
        for r in range(seq // rc):
            x = xp_ref[r * rc:(r + 1) * rc, :]
            a = jnp.dot(x, wa_s[...], preferred_element_type=F32)
            g = jnp.dot(x, wb_s[...], preferred_element_type=F32)
            a1 = jnp.where(rows >= 1, pltpu.roll(a, 1, 0), prev1)
            a2 = jnp.where(rows >= 2, pltpu.roll(a, 2, 0), jnp.where(rows == 1, prev1, prev2))
            y = jnp.where(valid, _conv_gate(a, a1, a2, g, cw_ref, cb_ref), 0.0)
            yp_ref[r * rc:(r + 1) * rc, :] = y.astype(yp_ref.dtype)
            prev1 = a[rc - 1:rc, :]
            prev2 = a[rc - 2:rc - 1, :]
            if r == seq // rc - 1:
                st_ref[0] = a[rc - SUBLANES:, :]


def ffn_in(hnp, hns, w_in_all, layer, cw, cb, buf, batch, seq):
    tn = FFN_TN
    d = hnp.shape[1]
    ts = hns.shape[0]
    pad = ((0, 0), (0, D_FF_PAD - D_FF))
    b0 = jnp.pad(buf[:, 0, :], pad)
    b1 = jnp.pad(buf[:, 1, :], pad)
    sq = lambda b: jnp.maximum(b - 1, 0)
    col = lambda rows_: pl.BlockSpec((rows_, tn), lambda j, b: (0, j))
    yp, st, ys, a_s = pl.pallas_call(
        _ffn_in_kernel,
        out_shape=(jax.ShapeDtypeStruct((batch * seq, D_FF_PAD), BF16),
                   jax.ShapeDtypeStruct((batch, SUBLANES, D_FF_PAD), F32),
                   jax.ShapeDtypeStruct((ts, D_FF_PAD), BF16),
                   jax.ShapeDtypeStruct((ts, D_FF_PAD), F32)),
        grid=(D_FF_PAD // tn, batch + 1),
        in_specs=[pl.BlockSpec((seq, d), lambda j, b: (sq(b), 0)), pl.BlockSpec((ts, d), lambda j, b: (0, 0))]
        + _ffn_weight_specs(layer, d, lambda j, b: j)
        + [col(CONV_W), col(1), col(ts), col(ts)],
        out_specs=(pl.BlockSpec((seq, tn), lambda j, b: (sq(b), j)),
                   pl.BlockSpec((1, SUBLANES, tn), lambda j, b: (sq(b), 0, j)),
                   col(ts), col(ts)),
        scratch_shapes=[pltpu.VMEM((d, tn), BF16), pltpu.VMEM((d, tn), BF16)],
        compiler_params=_params("arbitrary", "arbitrary"),
        name="ffn_in",
    )(hnp, hns, *([w_in_all] * (2 * FFN_W_BLOCKS)), cw, cb, b0, b1)
    conv_p = st[:, SUBLANES - (CONV_W - 1):, :D_FF]
    conv_s = jnp.stack([buf[:, 1, :], a_s[:, :D_FF]], axis=1)
    return yp, ys, conv_p, conv_s


def _ple_kernel(hnp_ref, hns_ref, wg_ref, pp_ref, ps_ref, wp_ref, hp_ref, hs_ref, g_ref, *refs, emit_h, n_blocks):
    out_refs, (wg_s, wp_s) = refs[:-2], refs[-2:]
    n_out = len(out_refs) // 2
    i = pl.program_id(0)

    @pl.when(i == 0)
    def _():
        wg_s[...] = wg_ref[...].astype(BF16)
        wp_s[...] = wp_ref[...].astype(BF16)

    def run(hn_ref, p_ref, h_ref, outs):
        gate = jax.nn.sigmoid(jnp.dot(hn_ref[...], wg_s[...], preferred_element_type=F32))
        proj = jnp.dot(p_ref[...].astype(BF16), wp_s[...], preferred_element_type=F32)
        h_new = h_ref[...] + gate * proj
        if emit_h:
            outs[0][...] = h_new
        outs[-1][...] = _rms(h_new, g_ref[...]).astype(outs[-1].dtype)

    @pl.when(i < n_blocks)
    def _():
        run(hnp_ref, pp_ref, hp_ref, out_refs[:n_out])

    @pl.when(i == n_blocks)
    def _():
        run(hns_ref, ps_ref, hs_ref, out_refs[n_out:])


def ple_residual(hnp, hns, wg_all, pp_all, ps_all, layer, wp_all, hp, hs, g_next, emit_h):
    tp, d = hp.shape
    ts = hs.shape[0]
    pd = pp_all.shape[-1]
    pp_all = pp_all.reshape(pp_all.shape[0], tp, pd)
    ps_all = ps_all.reshape(ps_all.shape[0], ts, pd)
    tm = PLE_TM
    nb = tp // tm
    blk = lambda i: jnp.minimum(i, nb - 1)
    prow = lambda width: pl.BlockSpec((tm, width), lambda i: (blk(i), 0))
    srow = lambda width: pl.BlockSpec((ts, width), lambda i: (0, 0))
    const = lambda r, c: pl.BlockSpec((None, r, c), lambda i: (layer, 0, 0), pipeline_mode=pl.Buffered(1))
    if emit_h:
        out_shape = tuple(jax.ShapeDtypeStruct((rows, d), dt) for rows in (tp, ts) for dt in (F32, BF16))
        out_specs = (prow(d), prow(d), srow(d), srow(d))
    else:
        out_shape = (jax.ShapeDtypeStruct((tp, d), F32), jax.ShapeDtypeStruct((ts, d), F32))
        out_specs = (prow(d), srow(d))
    return pl.pallas_call(
        functools.partial(_ple_kernel, emit_h=emit_h, n_blocks=nb),
        out_shape=out_shape,
        grid=(nb + 1,),
        in_specs=[prow(d), srow(d), const(d, d),
                  pl.BlockSpec((None, tm, pd), lambda i: (layer, blk(i), 0)),
                  pl.BlockSpec((None, ts, pd), lambda i: (layer, 0, 0)),
                  const(pd, d), prow(d), srow(d), pl.BlockSpec((1, d), lambda i: (0, 0))],
        out_specs=out_specs,
        scratch_shapes=[pltpu.VMEM((d, d), BF16), pltpu.VMEM((pd, d), BF16)],
        compiler_params=_params("arbitrary"),
        name="ple_residual",
    )(hnp, hns, wg_all, pp_all, ps_all, wp_all, hp, hs, g_next.reshape(1, d))


def _trunks(x_prompt, x_sample, state_ret, state_conv, p_prompt, p_sample, w):
    bp, lp = x_prompt.shape[0], x_prompt.shape[1]
    bs, ls = x_sample.shape[0], x_sample.shape[1]
    hp = x_prompt.reshape(bp * lp, D_MODEL)
    hs = x_sample.reshape(bs * ls, D_MODEL)
    hnp = rmsnorm_rows(hp, w["norm_mix_g"][0], BF16)
    hns = rmsnorm_rows(hs, w["norm_mix_g"][0], BF16)
    ret_p, conv_p, conv_s, v_s = [], [], [], []
    ret_s = None
    for i in range(DEPTH):
        li = i // N_MIXERS
        if i % N_MIXERS == 0:
            zp, zs = matmul_act(hnp, hns, w["gmlp_w_in"], li, "gelu", BF16, "gmlp_in")
            gate_w = (w["gmlp_ln_g"][li], w["gmlp_ln_b"][li], w["gmlp_w_s"][li], w["gmlp_b_s"][li])
            xmp = gmlp_gate_prompt(zp, *gate_w)
            xms, vn = gmlp_gate_sample(zs, *gate_w)
            v_s.append(vn.reshape(bs, ls, GMLP_WIDTH))
            w_out = w["gmlp_w_out"]
        else:
            rotary = (_rotary_tables(jnp.arange(lp, dtype=jnp.int32)),
                      _rotary_tables(PAST_LEN + jnp.arange(ls, dtype=jnp.int32)))
            qkp, qks = matmul_act(hnp, hns, w["ret_w_in"], li, None, F32, "ret_in_qk",
                                  cols=(0, 2 * RET_HK), rotary=rotary)
            vgp, vgs = matmul_act(hnp, hns, w["ret_w_in"], li, None, F32, "ret_in_vg",
                                  cols=(2 * RET_HK, 2 * RET_HV))
            xmp, s_new, xms, ret_s = retention(qkp, vgp, jnp.concatenate([qks, vgs], axis=1), bp, lp,
                                               state_ret, li, ret_s)
            ret_p.append(s_new)
            w_out = w["ret_w_out"]
        hp, hnp, hs, hns = proj_residual(xmp, xms, w_out, li, hp, hs, w["norm_ffn_g"][i],
                                         MIXER_OUT_W_CHUNK, "mixer_out")
        yp, ys, cp, cs = ffn_in(hnp, hns, w["ffn_w_in"], i, w["ffn_cw"][i], w["ffn_cb"][i], state_conv[i], bp, lp)
        conv_p.append(cp)
        conv_s.append(cs)
        hp, hnp, hs, hns = proj_residual(yp, ys, w["ffn_w_down"], i, hp, hs, w["norm_ple_g"][i],
                                         FFN_DOWN_W_CHUNK, "ffn_down")
        last = i == DEPTH - 1
        g_next = w["final_norm_g"] if last else w["norm_mix_g"][i + 1]
        outs = ple_residual(hnp, hns, w["ple_w_gate"], p_prompt, p_sample, i, w["ple_w_proj"], hp, hs, g_next,
                            not last)
        if last:
            y_p, y_s = outs
        else:
            hp, hnp, hs, hns = outs
    return (y_p.reshape(bp, lp, D_MODEL), y_s.reshape(bs, ls, D_MODEL),
            jnp.stack(ret_p), ret_s.reshape(state_ret.shape),
            jnp.stack(conv_p), jnp.stack(conv_s), jnp.stack(v_s))


def kernel(x_prompt, x_sample, state_ret, state_conv, p_prompt, p_sample, norm_mix_g, norm_ffn_g, norm_ple_g, final_norm_g, gmlp_w_in, gmlp_ln_g, gmlp_ln_b, gmlp_w_s, gmlp_b_s, gmlp_w_out, ret_w_in, ret_w_out, ffn_w_in, ffn_conv_w, ffn_conv_b, ffn_w_down, ple_w_proj, ple_w_gate):
    ff_pad = D_FF_PAD - D_FF
    w = {
        "norm_mix_g": norm_mix_g, "norm_ffn_g": norm_ffn_g, "norm_ple_g": norm_ple_g, "final_norm_g": final_norm_g,
        "gmlp_w_in": gmlp_w_in, "gmlp_ln_g": gmlp_ln_g, "gmlp_ln_b": gmlp_ln_b,
        "gmlp_w_s": gmlp_w_s, "gmlp_b_s": gmlp_b_s, "gmlp_w_out": gmlp_w_out,
        "ret_w_in": ret_w_in, "ret_w_out": ret_w_out,
        "ffn_w_in": ffn_w_in,
        "ffn_cw": jnp.pad(ffn_conv_w, ((0, 0), (0, 0), (0, ff_pad))),
        "ffn_cb": jnp.pad(ffn_conv_b, ((0, 0), (0, ff_pad))).reshape(DEPTH, 1, D_FF_PAD),
        "ffn_w_down": ffn_w_down,
        "ple_w_proj": ple_w_proj, "ple_w_gate": ple_w_gate,
    }
    return _trunks(x_prompt, x_sample, state_ret, state_conv, p_prompt, p_sample, w)
```
